```python
import jax, jax.numpy as jnp
from jax import lax
import numpy as np

D_MODEL = 2048
BATCH = 4
SEQ = 2048
DEPTH = 1
DEC_BATCH = 1
DEC_SEQ = 8192
PAST_LEN = 128

N_META = 16
GLA_HEADS = 4
GLA_DQK = D_MODEL // 2
GLA_DV_TOT = D_MODEL
GLA_DK = GLA_DQK // GLA_HEADS
GLA_DV = GLA_DV_TOT // GLA_HEADS
GLA_GATE_RANK = 16
GLA_TAU = 16.0
GLA_CHUNK = 64
CHUNK_PAD = GLA_CHUNK - N_META
DECAY_BIAS_MEAN = 2.0
CONV_CH = D_MODEL
CONV_W = 31
D_FF = 4 * D_MODEL
EPS = 1e-6

IN_SIZES = (GLA_DQK, GLA_DQK, GLA_DV_TOT, GLA_DV_TOT, GLA_GATE_RANK, GLA_GATE_RANK, 2 * CONV_CH, 2 * D_MODEL)
N_IN = int(sum(IN_SIZES))
IN_SPLITS = tuple(int(s) for s in np.cumsum(IN_SIZES)[:-1])

kernel_name = "hybrid_gla_conformer_encoder"


def _rmsnorm(x, g):
    xf = x.astype(jnp.float32)
    y = xf * lax.rsqrt(jnp.mean(xf * xf, axis=-1, keepdims=True) + EPS)
    return (y * g.astype(jnp.float32)).astype(x.dtype)


def _gla_chunked(q, k, v, log_a, inclusive):
    bsz, nh = q.shape[0], q.shape[1]
    b = jnp.cumsum(log_a, axis=3)
    b_last = b[:, :, :, -1:, :]
    q_d = q * jnp.exp(b)
    k_d = k * jnp.exp(-b)
    k_end = k * jnp.exp(b_last - b)
    scores = jnp.einsum('bhnik,bhnjk->bhnij', q_d, k_d)
    mask = jnp.tril(jnp.ones((GLA_CHUNK, GLA_CHUNK), dtype=bool), 0 if inclusive else -1)
    scores = jnp.where(mask, scores, 0.0)
    o_intra = jnp.einsum('bhnij,bhnjv->bhniv', scores, v)
    chunk_decay = jnp.exp(b_last[:, :, :, 0, :])

    def step(state, inp):
        q_c, k_c, v_c, dec_c = inp
        o_c = jnp.einsum('bhik,bhkv->bhiv', q_c, state)
        state = dec_c[..., None] * state + jnp.einsum('bhjk,bhjv->bhkv', k_c, v_c)
        return state, o_c

    init = jnp.zeros((bsz, nh, GLA_DK, GLA_DV), jnp.float32)
    xs = (jnp.moveaxis(q_d, 2, 0), jnp.moveaxis(k_end, 2, 0), jnp.moveaxis(v, 2, 0), jnp.moveaxis(chunk_decay, 2, 0))
    _, o_inter = lax.scan(step, init, xs)
    return o_intra + jnp.moveaxis(o_inter, 0, 2)


def _gla_branch(u_q, u_k, u_v, u_r, z_f, z_b, w_a2_f, b_a_f, w_a2_b, b_a_b, g_gla, w_gla_o):
    bsz, seq_len, _ = u_q.shape
    lp = seq_len + CHUNK_PAD
    n_chunks = lp // GLA_CHUNK
    f32 = jnp.float32
    la_f = jax.nn.log_sigmoid((z_f @ w_a2_f + b_a_f).astype(f32)) / GLA_TAU
    la_b = jax.nn.log_sigmoid((z_b @ w_a2_b + b_a_b).astype(f32)) / GLA_TAU
    pad = lambda t: jnp.pad(t, ((0, 0), (CHUNK_PAD, 0), (0, 0)))
    q = pad(u_q.astype(f32) * (GLA_DK ** -0.5))
    k = pad(u_k.astype(f32))
    v = pad(u_v.astype(f32))
    la_f = pad(la_f)
    la_b = pad(la_b)

    def to_chunks(t, d):
        return t.reshape(bsz, n_chunks, GLA_CHUNK, GLA_HEADS, d).transpose(0, 3, 1, 2, 4)

    def from_chunks(t):
        return t.transpose(0, 2, 3, 1, 4).reshape(bsz, lp, GLA_HEADS, GLA_DV)

    flip = lambda t: jnp.flip(t, axis=1)
    o_f = from_chunks(_gla_chunked(to_chunks(q, GLA_DK), to_chunks(k, GLA_DK), to_chunks(v, GLA_DV),
                                   to_chunks(la_f, GLA_DK), True))
    o_b = flip(from_chunks(_gla_chunked(to_chunks(flip(q), GLA_DK), to_chunks(flip(k), GLA_DK),
                                        to_chunks(flip(v), GLA_DV), to_chunks(flip(la_b), GLA_DK), False)))
    o = (o_f + o_b)[:, CHUNK_PAD:]
    o = o * lax.rsqrt(jnp.mean(o * o, axis=-1, keepdims=True) + EPS)
    o = o.reshape(bsz, seq_len, GLA_DV_TOT) * g_gla.astype(f32)
    o = o.astype(u_r.dtype) * jax.nn.silu(u_r)
    return o @ w_gla_o


def _conv_branch(p, w_dw, b_dw, ln_g, ln_b, w_conv_o, b_conv_o):
    a, gt = jnp.split(p, 2, axis=-1)
    g = a * jax.nn.sigmoid(gt)
    y = lax.conv_general_dilated(g, w_dw[:, None, :].astype(g.dtype), window_strides=(1,),
                                 padding=((CONV_W // 2, CONV_W // 2),),
                                 dimension_numbers=('NWC', 'WIO', 'NWC'),
                                 feature_group_count=CONV_CH) + b_dw
    yf = y.astype(jnp.float32)
    mu = jnp.mean(yf, axis=-1, keepdims=True)
    var = jnp.mean(jnp.square(yf - mu), axis=-1, keepdims=True)
    yf = (yf - mu) * lax.rsqrt(var + EPS) * ln_g.astype(jnp.float32) + ln_b.astype(jnp.float32)
    y = jax.nn.silu(yf).astype(p.dtype)
    return y @ w_conv_o + b_conv_o


def _layer(h, g_pre_mix, w_in, w_a2_f, b_a_f, w_a2_b, b_a_b, g_gla, w_gla_o,
           w_dw, b_dw, ln_g, ln_b, w_conv_o, b_conv_o, w_out, g_post_mix,
           g_pre_mlp, w_up, w_down, g_post_mlp):
    u = _rmsnorm(h, g_pre_mix)
    proj = u @ w_in
    u_q, u_k, u_v, u_r, z_f, z_b, p_glu, gate_logits = jnp.split(proj, IN_SPLITS, axis=-1)
    y_a = _gla_branch(u_q, u_k, u_v, u_r, z_f, z_b, w_a2_f, b_a_f, w_a2_b, b_a_b, g_gla, w_gla_o)
    y_b = _conv_branch(p_glu, w_dw, b_dw, ln_g, ln_b, w_conv_o, b_conv_o)
    gate_a, gate_b = jnp.split(jax.nn.sigmoid(gate_logits), 2, axis=-1)
    mix = (gate_a * y_a + gate_b * y_b) @ w_out
    h = h + _rmsnorm(mix, g_post_mix)
    u = _rmsnorm(h, g_pre_mlp)
    f = jnp.square(jax.nn.relu(u @ w_up)) @ w_down
    return h + _rmsnorm(f, g_post_mlp)


def _trunk(x, meta_tokens, layer_params):
    bsz = x.shape[0]
    meta = jnp.broadcast_to(meta_tokens[None].astype(x.dtype), (bsz, N_META, D_MODEL))
    h = jnp.concatenate([meta, x], axis=1)
    for l in range(DEPTH):
        h = _layer(h, *[p[l] for p in layer_params])
    return h[:, N_META:]


def setup_inputs(seed: int = 0) -> dict:
    key = jax.random.key(seed)
    ks = jax.random.split(key, 24)
    nrm = lambda k, shape, scale: jax.random.normal(k, shape, jnp.float32) * scale
    gain = lambda k, shape: 1.0 + nrm(k, shape, 0.02)
    D, L = D_MODEL, DEPTH
    return {
        "x_prompt": nrm(ks[0], (BATCH, SEQ, D), 1.0),
        "x_sample": nrm(ks[1], (DEC_BATCH, DEC_SEQ, D), 1.0),
        "meta_tokens": nrm(ks[2], (N_META, D), 1.0),
        "g_pre_mix": gain(ks[3], (L, D)),
        "w_in": nrm(ks[4], (L, D, N_IN), D ** -0.5),
        "w_a2_f": nrm(ks[5], (L, GLA_GATE_RANK, GLA_DQK), GLA_GATE_RANK ** -0.5),
        "b_a_f": DECAY_BIAS_MEAN + nrm(ks[6], (L, GLA_DQK), 0.5),
        "w_a2_b": nrm(ks[7], (L, GLA_GATE_RANK, GLA_DQK), GLA_GATE_RANK ** -0.5),
        "b_a_b": DECAY_BIAS_MEAN + nrm(ks[8], (L, GLA_DQK), 0.5),
        "g_gla": gain(ks[9], (L, GLA_DV_TOT)),
        "w_gla_o": nrm(ks[10], (L, GLA_DV_TOT, D), GLA_DV_TOT ** -0.5),
        "w_dw": nrm(ks[11], (L, CONV_W, CONV_CH), CONV_W ** -0.5),
        "b_dw": nrm(ks[12], (L, CONV_CH), 0.02),
        "ln_g": gain(ks[13], (L, CONV_CH)),
        "ln_b": nrm(ks[14], (L, CONV_CH), 0.02),
        "w_conv_o": nrm(ks[15], (L, CONV_CH, D), CONV_CH ** -0.5),
        "b_conv_o": nrm(ks[16], (L, D), 0.02),
        "w_out": nrm(ks[17], (L, D, D), D ** -0.5),
        "g_post_mix": gain(ks[18], (L, D)),
        "g_pre_mlp": gain(ks[19], (L, D)),
        "w_up": nrm(ks[20], (L, D, D_FF), D ** -0.5),
        "w_down": nrm(ks[21], (L, D_FF, D), D_FF ** -0.5),
        "g_post_mlp": gain(ks[22], (L, D)),
    }


def reference(x_prompt, x_sample, meta_tokens, g_pre_mix, w_in, w_a2_f, b_a_f, w_a2_b, b_a_b,
              g_gla, w_gla_o, w_dw, b_dw, ln_g, ln_b, w_conv_o, b_conv_o, w_out, g_post_mix,
              g_pre_mlp, w_up, w_down, g_post_mlp):
    layer_params = (g_pre_mix, w_in, w_a2_f, b_a_f, w_a2_b, b_a_b, g_gla, w_gla_o,
                    w_dw, b_dw, ln_g, ln_b, w_conv_o, b_conv_o, w_out, g_post_mix,
                    g_pre_mlp, w_up, w_down, g_post_mlp)
    y_prompt = _trunk(x_prompt, meta_tokens, layer_params)
    y_sample = _trunk(x_sample, meta_tokens, layer_params)
    return (y_prompt, y_sample)
```

```python
import functools

import jax
import jax.numpy as jnp
import numpy as np
from jax import lax
from jax.experimental import pallas as pl
from jax.experimental.pallas import tpu as pltpu

F32 = jnp.float32
BF16 = jnp.bfloat16

D_MODEL = 2048
N_META = 16
HEADS = 4
DK = 256
DV = 512
DQK = HEADS * DK
DV_TOT = HEADS * DV
GATE_RANK = 16
TAU = 16.0
CHUNK = 64
FRONT = CHUNK - N_META
CONV_W = 31
CONV_PAD = CONV_W // 2
D_FF = 4 * D_MODEL
EPS = 1e-6

LANES = 128
HALO = 16
N_CBLK = D_MODEL // LANES
N_MAIN = 2 * DQK + 2 * DV_TOT + 2 * D_MODEL + 2 * D_MODEL
COL_P = (2 * DQK + 2 * DV_TOT) // D_MODEL

GLA_BLOCK_CHUNKS = 3
GLA_ROWS = GLA_BLOCK_CHUNKS * CHUNK

VMEM_LIMIT = 56 * 1024 * 1024


def _sigmoid(x):
    return 1.0 / (1.0 + jnp.exp(-x))


def _rms(x, g):
    ms = jnp.mean(x * x, axis=-1, keepdims=True)
    return x * lax.rsqrt(ms + EPS) * g


def _in_proj_kernel(x_ref, g_ref, wz_ref, w_ref, o_ref, z_ref, u_ref):
    @pl.when(pl.program_id(1) == 0)
    def _():
        u = _rms(x_ref[...], g_ref[...]).astype(BF16)
        u_ref[...] = u
        z_ref[...] = jnp.dot(u, wz_ref[...], preferred_element_type=F32).astype(BF16)

    o_ref[...] = jnp.dot(u_ref[...], w_ref[...], preferred_element_type=F32).astype(BF16)


def _in_proj(h, g, w_main, w_z, tm, tn):
    t = h.shape[0]
    return pl.pallas_call(
        _in_proj_kernel,
        grid=(t // tm, N_MAIN // tn),
        in_specs=[
            pl.BlockSpec((tm, D_MODEL), lambda i, j: (i, 0)),
            pl.BlockSpec((1, D_MODEL), lambda i, j: (0, 0)),
            pl.BlockSpec((D_MODEL, LANES), lambda i, j: (0, 0)),
            pl.BlockSpec((D_MODEL, tn), lambda i, j: (0, j)),
        ],
        out_specs=[
            pl.BlockSpec((tm, tn), lambda i, j: (i, j)),
            pl.BlockSpec((tm, LANES), lambda i, j: (i, 0)),
        ],
        out_shape=[
            jax.ShapeDtypeStruct((t, N_MAIN), BF16),
            jax.ShapeDtypeStruct((t, LANES), BF16),
        ],
        scratch_shapes=[pltpu.VMEM((tm, D_MODEL), BF16)],
        compiler_params=pltpu.CompilerParams(
            dimension_semantics=("arbitrary", "arbitrary"),
            vmem_limit_bytes=VMEM_LIMIT),
        name="in_proj",
    )(h, g, w_z, w_main)


def _gla_kernel(blk_ref, first_ref, sblk_ref, q_ref, k_ref, v_ref, z_ref, wa_ref, ba_ref,
                *rest, reverse):
    if reverse:
        of_ref, r_ref, gg_ref, o_ref, s_ref = rest
    else:
        o_ref, s_ref = rest
    t = pl.program_id(1)

    @pl.when(first_ref[t] == 1)
    def _():
        s_ref[...] = jnp.zeros_like(s_ref)

    ri = lax.broadcasted_iota(jnp.int32, (CHUNK, CHUNK), 0)
    ci = lax.broadcasted_iota(jnp.int32, (CHUNK, CHUNK), 1)
    cum_mask = (ci >= ri) if reverse else (ci <= ri)
    score_mask = (ci > ri) if reverse else (ci <= ri)
    tri = jnp.where(cum_mask, 1.0, 0.0).astype(BF16)
    tri2 = jnp.concatenate([tri, tri], axis=1)
    pad_rows = jnp.where(sblk_ref[t] == 0, FRONT, 0)
    row_id = lax.broadcasted_iota(jnp.int32, (CHUNK, DK), 0)

    order = range(GLA_BLOCK_CHUNKS - 1, -1, -1) if reverse else range(GLA_BLOCK_CHUNKS)
    for c in order:
        sl = pl.ds(c * CHUNK, CHUNK)
        lin = jnp.dot(z_ref[sl, :], wa_ref[...], preferred_element_type=F32) + ba_ref[...]
        la = (jnp.minimum(lin, 0.0) - jnp.log1p(jnp.exp(-jnp.abs(lin)))) / TAU
        if c == 0:
            la = jnp.where(row_id >= pad_rows, la, 0.0)
        la_hi = la.astype(BF16)
        la_lo = (la - la_hi.astype(F32)).astype(BF16)
        b = jnp.dot(tri2, jnp.concatenate([la_hi, la_lo], axis=0), preferred_element_type=F32)
        b_tot = b[0:1, :] if reverse else b[CHUNK - 1:CHUNK, :]

        q = q_ref[sl, :].astype(F32) * (DK ** -0.5)
        k = k_ref[sl, :].astype(F32)
        v = v_ref[sl, :]
        q_d = (q * jnp.exp(b)).astype(BF16)
        k_d = (k * jnp.exp(-b)).astype(BF16)
        k_e = (k * jnp.exp(b_tot - b)).astype(BF16)

        scores = lax.dot_general(q_d, k_d, (((1,), (1,)), ((), ())), preferred_element_type=F32)
        scores = jnp.where(score_mask, scores, 0.0).astype(BF16)
        state = s_ref[...]
        o = (jnp.dot(scores, v, preferred_element_type=F32)
             + jnp.dot(q_d, state.astype(BF16), preferred_element_type=F32))

        dec = jnp.broadcast_to(jnp.exp(b_tot), (LANES, DK)).T
        upd = lax.dot_general(k_e, v, (((0,), (0,)), ((), ())), preferred_element_type=F32)
        s_ref[...] = state * jnp.concatenate([dec] * (DV // LANES), axis=1) + upd

        if reverse:
            o = o + of_ref[sl, :]
            o = _rms(o, gg_ref[...])
            r = r_ref[sl, :].astype(F32)
            o_ref[sl, :] = (o * (r * _sigmoid(r))).astype(o_ref.dtype)
        else:
            o_ref[sl, :] = o


def _gla(proj, z, wa, ba, tables, *, reverse, o_fwd=None, g_gla=None):
    t = proj.shape[0]
    blk, first, sblk = tables
    row = lambda h, s, blk_ref, first_ref, sblk_ref: blk_ref[s]
    in_specs = [
        pl.BlockSpec((GLA_ROWS, DK), lambda h, s, *p: (row(h, s, *p), h)),
        pl.BlockSpec((GLA_ROWS, DK), lambda h, s, *p: (row(h, s, *p), DQK // DK + h)),
        pl.BlockSpec((GLA_ROWS, DV), lambda h, s, *p: (row(h, s, *p), 2 * DQK // DV + h)),
        pl.BlockSpec((GLA_ROWS, LANES), lambda h, s, *p: (row(h, s, *p), 0)),
        pl.BlockSpec((LANES, DK), lambda h, s, *p: (0, h)),
        pl.BlockSpec((1, DK), lambda h, s, *p: (0, h)),
    ]
    args = [proj, proj, proj, z, wa, ba]
    if reverse:
        in_specs += [
            pl.BlockSpec((GLA_ROWS, DV), lambda h, s, *p: (row(h, s, *p), h)),
            pl.BlockSpec((GLA_ROWS, DV), lambda h, s, *p: (row(h, s, *p), (2 * DQK + DV_TOT) // DV + h)),
            pl.BlockSpec((1, DV), lambda h, s, *p: (0, h)),
        ]
        args += [o_fwd, proj, g_gla]
    return pl.pallas_call(
        functools.partial(_gla_kernel, reverse=reverse),
        grid_spec=pltpu.PrefetchScalarGridSpec(
            num_scalar_prefetch=3,
            grid=(HEADS, blk.shape[0]),
            in_specs=in_specs,
            out_specs=pl.BlockSpec((GLA_ROWS, DV), lambda h, s, *p: (row(h, s, *p), h)),
            scratch_shapes=[pltpu.VMEM((DK, DV), F32)],
        ),
        out_shape=jax.ShapeDtypeStruct((t, DV_TOT), BF16 if reverse else F32),
        compiler_params=pltpu.CompilerParams(
            dimension_semantics=("arbitrary", "arbitrary"),
            vmem_limit_bytes=VMEM_LIMIT),
        name="gla_bwd" if reverse else "gla_fwd",
    )(blk, first, sblk, *args)


def _gla_tables(seq_rows, reverse):
    blk, first, sblk = [], [], []
    base = 0
    for rows in seq_rows:
        nb = rows // GLA_ROWS
        order = range(nb - 1, -1, -1) if reverse else range(nb)
        for n, b in enumerate(order):
            blk.append(base + b)
            first.append(1 if n == 0 else 0)
            sblk.append(b)
        base += nb
    return tuple(jnp.asarray(np.array(a, np.int32)) for a in (blk, first, sblk))


def _mix_kernel(og_ref, a_ref, gt_ref, ap_ref, gtp_ref, an_ref, gtn_ref, ga_ref, gb_ref, h_ref,
                wgo_ref, wco_ref, wout_ref, wdw_ref, bdw_ref, lng_ref, lnb_ref, bco_ref,
                gpost_ref, gpre_ref, h1_ref, u2_ref, g_scr, y_scr, *, tm, rb):
    i = pl.program_id(0)
    n = pl.num_programs(0)

    def glu(a, gt):
        return a.astype(F32) * _sigmoid(gt.astype(F32))

    g_prev = jnp.where(i > 0, glu(ap_ref[...], gtp_ref[...]), 0.0)
    g_next = jnp.where(i < n - 1, glu(an_ref[...], gtn_ref[...]), 0.0)
    g_cur = glu(a_ref[...], gt_ref[...])
    for c in range(N_CBLK):
        cs = slice(c * LANES, (c + 1) * LANES)
        g_scr[c, 0:HALO, :] = g_prev[:, cs]
        g_scr[c, HALO:HALO + tm, :] = g_cur[:, cs]
        g_scr[c, HALO + tm:, :] = g_next[:, cs]

    def conv_block(c, carry):
        for r0 in range(0, tm, rb):
            acc = jnp.zeros((rb, LANES), F32)
            for w in range(CONV_W):
                start = HALO - CONV_PAD + w + r0
                acc = acc + g_scr[c, start:start + rb, :] * wdw_ref[c, w:w + 1, :]
            y_scr[c, r0:r0 + rb, :] = acc + bdw_ref[c]
        return carry

    lax.fori_loop(0, N_CBLK, conv_block, 0)

    s1 = jnp.zeros((tm, LANES), F32)
    for c in range(N_CBLK):
        s1 = s1 + y_scr[c]
    mu = jnp.sum(s1, axis=-1, keepdims=True) / D_MODEL
    s2 = jnp.zeros((tm, LANES), F32)
    for c in range(N_CBLK):
        d = y_scr[c] - mu
        s2 = s2 + d * d
    inv = lax.rsqrt(jnp.sum(s2, axis=-1, keepdims=True) / D_MODEL + EPS)
    pieces = []
    for c in range(N_CBLK):
        yn = (y_scr[c] - mu) * inv * lng_ref[c] + lnb_ref[c]
        pieces.append((yn * _sigmoid(yn)).astype(BF16))
    yc = jnp.concatenate(pieces, axis=1)

    y_b = jnp.dot(yc, wco_ref[...], preferred_element_type=F32) + bco_ref[...]
    y_a = jnp.dot(og_ref[...], wgo_ref[...], preferred_element_type=F32)
    merged = _sigmoid(ga_ref[...].astype(F32)) * y_a + _sigmoid(gb_ref[...].astype(F32)) * y_b
    mix = jnp.dot(merged.astype(BF16), wout_ref[...], preferred_element_type=F32)
    h1 = h_ref[...] + _rms(mix, gpost_ref[...])
    h1_ref[...] = h1
    u2_ref[...] = _rms(h1, gpre_ref[...]).astype(BF16)


def _mix(og, proj, h, wgo, wco, wout, wdw, bdw, lng, lnb, bco, gpost, gpre, tm, rb):
    t = h.shape[0]
    hb = tm // HALO
    last_hblk = t // HALO - 1
    tile = lambda col: pl.BlockSpec((tm, D_MODEL), lambda i: (i, col))
    prev = lambda col: pl.BlockSpec((HALO, D_MODEL), lambda i: (jnp.maximum(i * hb - 1, 0), col))
    nxt = lambda col: pl.BlockSpec((HALO, D_MODEL), lambda i: (jnp.minimum((i + 1) * hb, last_hblk), col))
    resident = lambda shape: pl.BlockSpec(shape, lambda i: (0,) * len(shape), pipeline_mode=pl.Buffered(1))
    vec = resident((1, D_MODEL))
    cvec = resident((N_CBLK, 1, LANES))
    return pl.pallas_call(
        functools.partial(_mix_kernel, tm=tm, rb=rb),
        grid=(t // tm,),
        in_specs=[
            tile(0),
            tile(COL_P), tile(COL_P + 1),
            prev(COL_P), prev(COL_P + 1),
            nxt(COL_P), nxt(COL_P + 1),
            tile(COL_P + 2), tile(COL_P + 3),
            tile(0),
            resident((DV_TOT, D_MODEL)), resident((D_MODEL, D_MODEL)), resident((D_MODEL, D_MODEL)),
            resident((N_CBLK, CONV_W + 1, LANES)), cvec, cvec, cvec, vec, vec, vec,
        ],
        out_specs=[tile(0), tile(0)],
        out_shape=[jax.ShapeDtypeStruct((t, D_MODEL), F32), jax.ShapeDtypeStruct((t, D_MODEL), BF16)],
        scratch_shapes=[
            pltpu.VMEM((N_CBLK, tm + 2 * HALO, LANES), F32),
            pltpu.VMEM((N_CBLK, tm, LANES), F32),
        ],
        compiler_params=pltpu.CompilerParams(
            dimension_semantics=("arbitrary",),
            vmem_limit_bytes=VMEM_LIMIT),
        name="mix",
    )(og, proj, proj, proj, proj, proj, proj, proj, proj, h,
      wgo, wco, wout, wdw, bdw, lng, lnb, bco, gpost, gpre)


def _mlp_kernel(u_ref, h_ref, wu_ref, wd_ref, g_ref, o_ref):
    f = pl.program_id(1)
    up = jnp.dot(u_ref[...], wu_ref[...], preferred_element_type=F32)
    act = jnp.square(jnp.maximum(up, 0.0)).astype(BF16)
    part = jnp.dot(act, wd_ref[...], preferred_element_type=F32)

    @pl.when(f == 0)
    def _():
        o_ref[...] = part

    @pl.when(f > 0)
    def _():
        o_ref[...] += part

    @pl.when(f == pl.num_programs(1) - 1)
    def _():
        o_ref[...] = h_ref[...] + _rms(o_ref[...], g_ref[...])


def _mlp(u2, h1, w_up, w_down, g, tm, tf):
    t = u2.shape[0]
    return pl.pallas_call(
        _mlp_kernel,
        grid=(t // tm, D_FF // tf),
        in_specs=[
            pl.BlockSpec((tm, D_MODEL), lambda i, f: (i, 0)),
            pl.BlockSpec((tm, D_MODEL), lambda i, f: (i, 0)),
            pl.BlockSpec((D_MODEL, tf), lambda i, f: (0, f)),
            pl.BlockSpec((tf, D_MODEL), lambda i, f: (f, 0)),
            pl.BlockSpec((1, D_MODEL), lambda i, f: (0, 0)),
        ],
        out_specs=pl.BlockSpec((tm, D_MODEL), lambda i, f: (i, 0)),
        out_shape=jax.ShapeDtypeStruct((t, D_MODEL), F32),
        compiler_params=pltpu.CompilerParams(
            dimension_semantics=("arbitrary", "arbitrary"),
            vmem_limit_bytes=VMEM_LIMIT),
        name="mlp",
    )(u2, h1, w_up, w_down, g)


def _pack_tokens(x, meta):
    bsz = x.shape[0]
    front = jnp.concatenate([jnp.zeros((FRONT, D_MODEL), x.dtype), meta.astype(x.dtype)], axis=0)
    front = jnp.broadcast_to(front[None], (bsz, CHUNK, D_MODEL))
    return jnp.concatenate([front, x], axis=1).reshape(-1, D_MODEL)


def _cblocks(vec):
    return vec.astype(F32).reshape(N_CBLK, 1, LANES)


def kernel(x_prompt, x_sample, meta_tokens, g_pre_mix, w_in, w_a2_f, b_a_f, w_a2_b, b_a_b, g_gla,
           w_gla_o, w_dw, b_dw, ln_g, ln_b, w_conv_o, b_conv_o, w_out, g_post_mix, g_pre_mlp,
           w_up, w_down, g_post_mlp):
    groups = (x_prompt, x_sample)
    seq_rows = [CHUNK + x.shape[1] for x in groups for _ in range(x.shape[0])]
    assert all(r % GLA_ROWS == 0 for r in seq_rows)
    h = jnp.concatenate([_pack_tokens(x, meta_tokens) for x in groups], axis=0)
    t = h.shape[0]
    tm_proj, tn_proj, tm_mix, rb_mix, tm_mlp, tf_mlp = 928, 1024, 288, 96, 576, 1024
    assert t % tm_proj == 0 and t % tm_mix == 0 and t % tm_mlp == 0

    tables_f = _gla_tables(seq_rows, reverse=False)
    tables_b = _gla_tables(seq_rows, reverse=True)
    row = lambda v: v.astype(F32).reshape(1, -1)
    z0 = 2 * DQK + 2 * DV_TOT

    assert w_in.shape[0] == 1
    l = 0
    w_main = jnp.concatenate([w_in[l][:, :z0], w_in[l][:, z0 + 2 * GATE_RANK:]], axis=1).astype(BF16)
    w_z = jnp.pad(w_in[l][:, z0:z0 + 2 * GATE_RANK], ((0, 0), (0, LANES - 2 * GATE_RANK))).astype(BF16)
    wa_f = jnp.pad(w_a2_f[l], ((0, LANES - GATE_RANK), (0, 0))).astype(BF16)
    wa_b = jnp.pad(w_a2_b[l], ((GATE_RANK, LANES - 2 * GATE_RANK), (0, 0))).astype(BF16)
    wdw = jnp.pad(w_dw[l].astype(F32), ((0, 1), (0, 0))).reshape(CONV_W + 1, N_CBLK, LANES).transpose(1, 0, 2)

    proj, z = _in_proj(h, row(g_pre_mix[l]), w_main, w_z, tm_proj, tn_proj)
    o_f = _gla(proj, z, wa_f, row(b_a_f[l]), tables_f, reverse=False)
    og = _gla(proj, z, wa_b, row(b_a_b[l]), tables_b, reverse=True, o_fwd=o_f, g_gla=row(g_gla[l]))
    h1, u2 = _mix(og, proj, h, w_gla_o[l].astype(BF16), w_conv_o[l].astype(BF16), w_out[l].astype(BF16),
                  wdw, _cblocks(b_dw[l]), _cblocks(ln_g[l]), _cblocks(ln_b[l]), row(b_conv_o[l]),
                  row(g_post_mix[l]), row(g_pre_mlp[l]), tm_mix, rb_mix)
    h = _mlp(u2, h1, w_up[l].astype(BF16), w_down[l].astype(BF16), row(g_post_mlp[l]), tm_mlp, tf_mlp)

    outs = []
    base = 0
    for x in groups:
        bsz, s = x.shape[0], x.shape[1]
        rows = bsz * (CHUNK + s)
        outs.append(h[base:base + rows].reshape(bsz, CHUNK + s, D_MODEL)[:, CHUNK:])
        base += rows
    return tuple(outs)
```

```python
import functools

import jax
import jax.numpy as jnp
import numpy as np
from jax import lax
from jax.experimental import pallas as pl
from jax.experimental.pallas import tpu as pltpu

F32 = jnp.float32
BF16 = jnp.bfloat16

D_MODEL = 2048
N_META = 16
HEADS = 4
DK = 256
DV = 512
DQK = HEADS * DK
DV_TOT = HEADS * DV
GATE_RANK = 16
TAU = 16.0
CHUNK = 64
FRONT = CHUNK - N_META
CONV_W = 31
CONV_PAD = CONV_W // 2
D_FF = 4 * D_MODEL
EPS = 1e-6

LANES = 128
HALO = 16
N_CBLK = D_MODEL // LANES
N_MAIN = 2 * DQK + 2 * DV_TOT + 2 * D_MODEL + 2 * D_MODEL
COL_R = (2 * DQK + DV_TOT) // D_MODEL
COL_P = (2 * DQK + 2 * DV_TOT) // D_MODEL

GLA_BLOCK_CHUNKS = 4
GLA_ROWS = GLA_BLOCK_CHUNKS * CHUNK

VMEM_LIMIT = 56 * 1024 * 1024

_NT = (((1,), (1,)), ((), ()))
_TN = (((0,), (0,)), ((), ()))


def _sigmoid(x):
    return 1.0 / (1.0 + jnp.exp(-x))


def _rms(x, g):
    ms = jnp.mean(x * x, axis=-1, keepdims=True)
    return x * lax.rsqrt(ms + EPS) * g


def _in_proj_kernel(x_ref, g_ref, wz_ref, w_ref, o_ref, z_ref, u_ref):
    @pl.when(pl.program_id(1) == 0)
    def _():
        u = _rms(x_ref[...], g_ref[...]).astype(BF16)
        u_ref[...] = u
        z_ref[...] = jnp.dot(u, wz_ref[...], preferred_element_type=F32).astype(BF16)

    o_ref[...] = jnp.dot(u_ref[...], w_ref[...], preferred_element_type=F32).astype(BF16)


def _in_proj(h, g, w_main, w_z, tm, tn):
    t = h.shape[0]
    return pl.pallas_call(
        _in_proj_kernel,
        grid=(t // tm, N_MAIN // tn),
        in_specs=[
            pl.BlockSpec((tm, D_MODEL), lambda i, j: (i, 0)),
            pl.BlockSpec((1, D_MODEL), lambda i, j: (0, 0)),
            pl.BlockSpec((D_MODEL, LANES), lambda i, j: (0, 0)),
            pl.BlockSpec((D_MODEL, tn), lambda i, j: (0, j)),
        ],
        out_specs=[
            pl.BlockSpec((tm, tn), lambda i, j: (i, j)),
            pl.BlockSpec((tm, LANES), lambda i, j: (i, 0)),
        ],
        out_shape=[
            jax.ShapeDtypeStruct((t, N_MAIN), BF16),
            jax.ShapeDtypeStruct((t, LANES), BF16),
        ],
        scratch_shapes=[pltpu.VMEM((tm, D_MODEL), BF16)],
        compiler_params=pltpu.CompilerParams(
            dimension_semantics=("arbitrary", "arbitrary"),
            vmem_limit_bytes=VMEM_LIMIT),
        name="in_proj",
    )(h, g, w_z, w_main)


def _cum_matrix(reverse):
    ri = lax.broadcasted_iota(jnp.int32, (CHUNK, CHUNK), 0)
    ci = lax.broadcasted_iota(jnp.int32, (CHUNK, CHUNK), 1)
    tri = jnp.where((ci >= ri) if reverse else (ci <= ri), 1.0, 0.0).astype(BF16)
    return jnp.concatenate([tri, tri], axis=1)


def _log_decay(z, wa, ba):
    lin = jnp.dot(z, wa, preferred_element_type=F32) + ba
    return (jnp.minimum(lin, 0.0) - jnp.log1p(jnp.exp(-jnp.abs(lin)))) / TAU


def _cum_decay(la, cum):
    la_hi = la.astype(BF16)
    la_lo = (la - la_hi.astype(F32)).astype(BF16)
    return jnp.dot(cum, jnp.concatenate([la_hi, la_lo], axis=0), preferred_element_type=F32)


def _gla_front_kernel(k_ref, v_ref, z_ref, wa_ref, ba_ref, s_ref):
    la = _log_decay(z_ref[...], wa_ref[...], ba_ref[...])
    row_id = lax.broadcasted_iota(jnp.int32, la.shape, 0)
    la = jnp.where(row_id >= FRONT, la, 0.0)
    b = _cum_decay(la, _cum_matrix(False))
    k_e = (k_ref[...].astype(F32) * jnp.exp(b[CHUNK - 1:CHUNK, :] - b)).astype(BF16)
    for h in range(HEADS):
        s_ref[h] = lax.dot_general(k_e[:, h * DK:(h + 1) * DK], v_ref[:, h * DV:(h + 1) * DV], _TN,
                                   preferred_element_type=F32)


def _gla_front(proj_front, z_front, wa, ba):
    return pl.pallas_call(
        _gla_front_kernel,
        grid=(1,),
        in_specs=[
            pl.BlockSpec((CHUNK, DQK), lambda i: (0, 1)),
            pl.BlockSpec((CHUNK, DV_TOT), lambda i: (0, 1)),
            pl.BlockSpec((CHUNK, LANES), lambda i: (0, 0)),
            pl.BlockSpec((LANES, DQK), lambda i: (0, 0)),
            pl.BlockSpec((1, DQK), lambda i: (0, 0)),
        ],
        out_specs=pl.BlockSpec((HEADS, DK, DV), lambda i: (0, 0, 0)),
        out_shape=jax.ShapeDtypeStruct((HEADS, DK, DV), F32),
        name="gla_front",
    )(proj_front, proj_front, z_front, wa, ba)


def _gla_kernel(blk_ref, first_ref, q_ref, k_ref, v_ref, z_ref, wa_ref, ba_ref, *rest, reverse):
    if reverse:
        of_ref, r_ref, gg_ref, o_ref, s_ref = rest
    else:
        s0_ref, o_ref, s_ref = rest
    t = pl.program_id(0)

    @pl.when(first_ref[t] == 1)
    def _():
        s_ref[...] = jnp.zeros_like(s_ref) if reverse else s0_ref[...]

    ri = lax.broadcasted_iota(jnp.int32, (CHUNK, CHUNK), 0)
    ci = lax.broadcasted_iota(jnp.int32, (CHUNK, CHUNK), 1)
    score_mask = (ci > ri) if reverse else (ci <= ri)
    cum = _cum_matrix(reverse)

    order = range(GLA_BLOCK_CHUNKS - 1, -1, -1) if reverse else range(GLA_BLOCK_CHUNKS)
    for c in order:
        sl = pl.ds(c * CHUNK, CHUNK)
        b = _cum_decay(_log_decay(z_ref[sl, :], wa_ref[...], ba_ref[...]), cum)
        b_tot = b[0:1, :] if reverse else b[CHUNK - 1:CHUNK, :]
        k = k_ref[sl, :].astype(F32)
        q_d = (q_ref[sl, :].astype(F32) * (DK ** -0.5) * jnp.exp(b)).astype(BF16)
        k_d = (k * jnp.exp(-b)).astype(BF16)
        k_e = (k * jnp.exp(b_tot - b)).astype(BF16)
        dec_rows = jnp.broadcast_to(jnp.exp(b_tot), (LANES, DQK))

        for h in range(HEADS):
            ks = slice(h * DK, (h + 1) * DK)
            vs = slice(h * DV, (h + 1) * DV)
            v = v_ref[sl, vs]
            scores = lax.dot_general(q_d[:, ks], k_d[:, ks], _NT, preferred_element_type=F32)
            scores = jnp.where(score_mask, scores, 0.0).astype(BF16)
            state = s_ref[h]
            o = (jnp.dot(scores, v, preferred_element_type=F32)
                 + jnp.dot(q_d[:, ks], state.astype(BF16), preferred_element_type=F32))
            dec = dec_rows[:, ks].T
            upd = lax.dot_general(k_e[:, ks], v, _TN, preferred_element_type=F32)
            s_ref[h] = state * jnp.concatenate([dec] * (DV // LANES), axis=1) + upd

            if reverse:
                o = _rms(o + of_ref[sl, vs], gg_ref[:, vs])
                r = r_ref[sl, vs].astype(F32)
                o_ref[sl, vs] = (o * (r * _sigmoid(r))).astype(o_ref.dtype)
            else:
                o_ref[sl, vs] = o


def _gla(proj, z, wa, ba, tables, *, reverse, s_front=None, o_fwd=None, g_gla=None):
    t = proj.shape[0]
    blk, first = tables
    rows = lambda width, col: pl.BlockSpec((GLA_ROWS, width), lambda s, blk_ref, first_ref: (blk_ref[s], col))
    const = lambda shape: pl.BlockSpec(shape, lambda s, blk_ref, first_ref: (0,) * len(shape))
    in_specs = [rows(DQK, 0), rows(DQK, 1), rows(DV_TOT, 1), rows(LANES, 0), const((LANES, DQK)), const((1, DQK))]
    args = [proj, proj, proj, z, wa, ba]
    if reverse:
        in_specs += [rows(DV_TOT, 0), rows(DV_TOT, COL_R), const((1, DV_TOT))]
        args += [o_fwd, proj, g_gla]
    else:
        in_specs += [const((HEADS, DK, DV))]
        args += [s_front]
    return pl.pallas_call(
        functools.partial(_gla_kernel, reverse=reverse),
        grid_spec=pltpu.PrefetchScalarGridSpec(
            num_scalar_prefetch=2,
            grid=(blk.shape[0],),
            in_specs=in_specs,
            out_specs=rows(DV_TOT, 0),
            scratch_shapes=[pltpu.VMEM((HEADS, DK, DV), F32)],
        ),
        out_shape=jax.ShapeDtypeStruct((t, DV_TOT), BF16 if reverse else F32),
        compiler_params=pltpu.CompilerParams(
            dimension_semantics=("arbitrary",),
            vmem_limit_bytes=VMEM_LIMIT),
        name="gla_bwd" if reverse else "gla_fwd",
    )(blk, first, *args)


def _gla_tables(n_seq, seq_len, reverse):
    nb = seq_len // GLA_ROWS
    order = np.arange(nb - 1, -1, -1) if reverse else np.arange(nb)
    blk = np.concatenate([s * nb + order for s in range(n_seq)]).astype(np.int32)
    first = np.tile(np.arange(nb) == 0, n_seq).astype(np.int32)
    return jnp.asarray(blk), jnp.asarray(first)


def _mix_kernel(og_ref, a_ref, gt_ref, ap_ref, gtp_ref, an_ref, gtn_ref, af_ref, gtf_ref, ga_ref, gb_ref,
                h_ref, wgo_ref, wco_ref, wout_ref, wdw_ref, bdw_ref, lng_ref, lnb_ref, bco_ref,
                gpost_ref, gpre_ref, h1_ref, u2_ref, g_scr, y_scr, *, tm, rb, tiles_per_seq):
    pos = pl.program_id(0) % tiles_per_seq
    seq_start = pos == 0
    seq_end = pos == tiles_per_seq - 1

    def glu(a, gt):
        return a.astype(F32) * _sigmoid(gt.astype(F32))

    g_prev = glu(jnp.where(seq_start, af_ref[...], ap_ref[...]), jnp.where(seq_start, gtf_ref[...], gtp_ref[...]))
    g_next = jnp.where(seq_end, 0.0, glu(an_ref[...], gtn_ref[...]))
    g_cur = glu(a_ref[...], gt_ref[...])
    for c in range(N_CBLK):
        cs = slice(c * LANES, (c + 1) * LANES)
        g_scr[c, 0:HALO, :] = g_prev[:, cs]
        g_scr[c, HALO:HALO + tm, :] = g_cur[:, cs]
        g_scr[c, HALO + tm:, :] = g_next[:, cs]

    def conv_block(c, carry):
        for r0 in range(0, tm, rb):
            acc = jnp.zeros((rb, LANES), F32)
            for w in range(CONV_W):
                start = HALO - CONV_PAD + w + r0
                acc = acc + g_scr[c, start:start + rb, :] * wdw_ref[c, w:w + 1, :]
            y_scr[c, r0:r0 + rb, :] = acc + bdw_ref[c]
        return carry

    lax.fori_loop(0, N_CBLK, conv_block, 0)

    s1 = jnp.zeros((tm, LANES), F32)
    for c in range(N_CBLK):
        s1 = s1 + y_scr[c]
    mu = jnp.sum(s1, axis=-1, keepdims=True) / D_MODEL
    s2 = jnp.zeros((tm, LANES), F32)
    for c in range(N_CBLK):
        d = y_scr[c] - mu
        s2 = s2 + d * d
    inv = lax.rsqrt(jnp.sum(s2, axis=-1, keepdims=True) / D_MODEL + EPS)
    pieces = []
    for c in range(N_CBLK):
        yn = (y_scr[c] - mu) * inv * lng_ref[c] + lnb_ref[c]
        pieces.append((yn * _sigmoid(yn)).astype(BF16))
    yc = jnp.concatenate(pieces, axis=1)

    y_b = jnp.dot(yc, wco_ref[...], preferred_element_type=F32) + bco_ref[...]
    y_a = jnp.dot(og_ref[...], wgo_ref[...], preferred_element_type=F32)
    merged = _sigmoid(ga_ref[...].astype(F32)) * y_a + _sigmoid(gb_ref[...].astype(F32)) * y_b
    mix = jnp.dot(merged.astype(BF16), wout_ref[...], preferred_element_type=F32)
    h1 = h_ref[...] + _rms(mix, gpost_ref[...])
    h1_ref[...] = h1
    u2_ref[...] = _rms(h1, gpre_ref[...]).astype(BF16)


def _mix(og, proj, proj_front, h, seq_len, wgo, wco, wout, wdw, bdw, lng, lnb, bco, gpost, gpre, tm, rb):
    t = h.shape[0]
    hb = tm // HALO
    last_hblk = t // HALO - 1
    tile = lambda col: pl.BlockSpec((tm, D_MODEL), lambda i: (i, col))
    prev = lambda col: pl.BlockSpec((HALO, D_MODEL), lambda i: (jnp.maximum(i * hb - 1, 0), col))
    nxt = lambda col: pl.BlockSpec((HALO, D_MODEL), lambda i: (jnp.minimum((i + 1) * hb, last_hblk), col))
    meta = lambda col: pl.BlockSpec((HALO, D_MODEL), lambda i: (FRONT // HALO, col))
    resident = lambda shape: pl.BlockSpec(shape, lambda i: (0,) * len(shape), pipeline_mode=pl.Buffered(1))
    vec = resident((1, D_MODEL))
    cvec = resident((N_CBLK, 1, LANES))
    return pl.pallas_call(
        functools.partial(_mix_kernel, tm=tm, rb=rb, tiles_per_seq=seq_len // tm),
        grid=(t // tm,),
        in_specs=[
            tile(0),
            tile(COL_P), tile(COL_P + 1),
            prev(COL_P), prev(COL_P + 1),
            nxt(COL_P), nxt(COL_P + 1),
            meta(COL_P), meta(COL_P + 1),
            tile(COL_P + 2), tile(COL_P + 3),
            tile(0),
            resident((DV_TOT, D_MODEL)), resident((D_MODEL, D_MODEL)), resident((D_MODEL, D_MODEL)),
            resident((N_CBLK, CONV_W + 1, LANES)), cvec, cvec, cvec, vec, vec, vec,
        ],
        out_specs=[tile(0), tile(0)],
        out_shape=[jax.ShapeDtypeStruct((t, D_MODEL), F32), jax.ShapeDtypeStruct((t, D_MODEL), BF16)],
        scratch_shapes=[
            pltpu.VMEM((N_CBLK, tm + 2 * HALO, LANES), F32),
            pltpu.VMEM((N_CBLK, tm, LANES), F32),
        ],
        compiler_params=pltpu.CompilerParams(
            dimension_semantics=("arbitrary",),
            vmem_limit_bytes=VMEM_LIMIT),
        name="mix",
    )(og, proj, proj, proj, proj, proj, proj, proj_front, proj_front, proj, proj, h,
      wgo, wco, wout, wdw, bdw, lng, lnb, bco, gpost, gpre)


def _mlp_kernel(u_ref, h_ref, wu_ref, wd_ref, g_ref, o_ref):
    f = pl.program_id(1)
    up = jnp.dot(u_ref[...], wu_ref[...], preferred_element_type=F32)
    act = jnp.square(jnp.maximum(up, 0.0)).astype(BF16)
    part = jnp.dot(act, wd_ref[...], preferred_element_type=F32)

    @pl.when(f == 0)
    def _():
        o_ref[...] = part

    @pl.when(f > 0)
    def _():
        o_ref[...] += part

    @pl.when(f == pl.num_programs(1) - 1)
    def _():
        o_ref[...] = h_ref[...] + _rms(o_ref[...], g_ref[...])


def _mlp(u2, h1, w_up, w_down, g, tm, tf):
    t = u2.shape[0]
    return pl.pallas_call(
        _mlp_kernel,
        grid=(t // tm, D_FF // tf),
        in_specs=[
            pl.BlockSpec((tm, D_MODEL), lambda i, f: (i, 0)),
            pl.BlockSpec((tm, D_MODEL), lambda i, f: (i, 0)),
            pl.BlockSpec((D_MODEL, tf), lambda i, f: (0, f)),
            pl.BlockSpec((tf, D_MODEL), lambda i, f: (f, 0)),
            pl.BlockSpec((1, D_MODEL), lambda i, f: (0, 0)),
        ],
        out_specs=pl.BlockSpec((tm, D_MODEL), lambda i, f: (i, 0)),
        out_shape=jax.ShapeDtypeStruct((t, D_MODEL), F32),
        compiler_params=pltpu.CompilerParams(
            dimension_semantics=("arbitrary", "arbitrary"),
            vmem_limit_bytes=VMEM_LIMIT),
        name="mlp",
    )(u2, h1, w_up, w_down, g)


def _cblocks(vec):
    return vec.astype(F32).reshape(N_CBLK, 1, LANES)


def kernel(x_prompt, x_sample, meta_tokens, g_pre_mix, w_in, w_a2_f, b_a_f, w_a2_b, b_a_b, g_gla,
           w_gla_o, w_dw, b_dw, ln_g, ln_b, w_conv_o, b_conv_o, w_out, g_post_mix, g_pre_mlp,
           w_up, w_down, g_post_mlp):
    assert w_in.shape[0] == 1
    l = 0
    tm_proj, tn_proj, tm_mix, rb_mix, tm_mlp, tf_mlp = 1024, 1024, 256, 64, 512, 1024
    row = lambda v: v.astype(F32).reshape(1, -1)
    z0 = 2 * DQK + 2 * DV_TOT

    w_main = jnp.concatenate([w_in[l][:, :z0], w_in[l][:, z0 + 2 * GATE_RANK:]], axis=1).astype(BF16)
    w_z = jnp.pad(w_in[l][:, z0:z0 + 2 * GATE_RANK], ((0, 0), (0, LANES - 2 * GATE_RANK))).astype(BF16)
    wa_f = jnp.pad(w_a2_f[l], ((0, LANES - GATE_RANK), (0, 0))).astype(BF16)
    wa_b = jnp.pad(w_a2_b[l], ((GATE_RANK, LANES - 2 * GATE_RANK), (0, 0))).astype(BF16)
    wdw = jnp.pad(w_dw[l].astype(F32), ((0, 1), (0, 0))).reshape(CONV_W + 1, N_CBLK, LANES).transpose(1, 0, 2)
    w_go, w_co, w_o = w_gla_o[l].astype(BF16), w_conv_o[l].astype(BF16), w_out[l].astype(BF16)
    w_u, w_d = w_up[l].astype(BF16), w_down[l].astype(BF16)

    front = jnp.concatenate([jnp.zeros((FRONT, D_MODEL), F32), meta_tokens.astype(F32)], axis=0)
    proj_front, z_front = _in_proj(front, row(g_pre_mix[l]), w_main, w_z, CHUNK, tn_proj)
    s_front = _gla_front(proj_front, z_front, wa_f, row(b_a_f[l]))

    outs = []
    for x in (x_prompt, x_sample):
        bsz, seq_len, _ = x.shape
        assert seq_len % GLA_ROWS == 0 and seq_len % tm_mix == 0
        h = x.reshape(bsz * seq_len, D_MODEL)
        proj, z = _in_proj(h, row(g_pre_mix[l]), w_main, w_z, tm_proj, tn_proj)
        o_f = _gla(proj, z, wa_f, row(b_a_f[l]), _gla_tables(bsz, seq_len, False), reverse=False, s_front=s_front)
        og = _gla(proj, z, wa_b, row(b_a_b[l]), _gla_tables(bsz, seq_len, True), reverse=True,
                  o_fwd=o_f, g_gla=row(g_gla[l]))
        h1, u2 = _mix(og, proj, proj_front, h, seq_len, w_go, w_co, w_o, wdw, _cblocks(b_dw[l]),
                      _cblocks(ln_g[l]), _cblocks(ln_b[l]), row(b_conv_o[l]), row(g_post_mix[l]),
                      row(g_pre_mlp[l]), tm_mix, rb_mix)
        y = _mlp(u2, h1, w_u, w_d, row(g_post_mlp[l]), tm_mlp, tf_mlp)
        outs.append(y.reshape(bsz, seq_len, D_MODEL))
    return tuple(outs)
```

```python
import functools

import jax
import jax.numpy as jnp
import numpy as np
from jax import lax
from jax.experimental import pallas as pl
from jax.experimental.pallas import tpu as pltpu

F32 = jnp.float32
BF16 = jnp.bfloat16

D_MODEL = 2048
N_META = 16
HEADS = 4
DK = 256
DV = 512
DQK = HEADS * DK
DV_TOT = HEADS * DV
GATE_RANK = 16
TAU = 16.0
CHUNK = 64
FRONT = CHUNK - N_META
CONV_W = 31
CONV_PAD = CONV_W // 2
D_FF = 4 * D_MODEL
EPS = 1e-6

LANES = 128
MXU_DIM = 256
HALO = 16
N_CBLK = D_MODEL // LANES
N_SLAB = D_MODEL // MXU_DIM
CBLK_PER_SLAB = MXU_DIM // LANES
N_HEAD_COLS = 2 * DQK + 2 * DV_TOT
N_TAIL_COLS = 2 * D_MODEL + 2 * D_MODEL
N_MAIN = N_HEAD_COLS + N_TAIL_COLS
COL_R = (2 * DQK + DV_TOT) // D_MODEL
COL_P = N_HEAD_COLS // D_MODEL

GLA_CHUNK = 2 * CHUNK
GLA_BLOCK_CHUNKS = 2
GLA_ROWS = GLA_BLOCK_CHUNKS * GLA_CHUNK

VMEM_LIMIT = 58 * 1024 * 1024

_NT = (((1,), (1,)), ((), ()))
_TN = (((0,), (0,)), ((), ()))


def _sigmoid(x):
    return 1.0 / (1.0 + jnp.exp(-x))


def _rms(x, g):
    ms = jnp.mean(x * x, axis=-1, keepdims=True)
    return x * lax.rsqrt(ms + EPS) * g


def _in_proj_kernel(x_ref, g_ref, wz_ref, wh_ref, wt_ref, o_ref, z_ref, u_ref, *, n_head_tiles):
    j = pl.program_id(1)

    @pl.when(j == 0)
    def _():
        u = _rms(x_ref[...], g_ref[...]).astype(BF16)
        u_ref[...] = u
        z_ref[...] = jnp.dot(u, wz_ref[...], preferred_element_type=F32).astype(BF16)

    @pl.when(j < n_head_tiles)
    def _():
        o_ref[...] = jnp.dot(u_ref[...], wh_ref[...], preferred_element_type=F32).astype(BF16)

    @pl.when(j >= n_head_tiles)
    def _():
        o_ref[...] = jnp.dot(u_ref[...], wt_ref[...], preferred_element_type=F32).astype(BF16)


def _in_proj(h, g, w_head, w_tail, w_z, tm, tn):
    t = h.shape[0]
    n_head_tiles = N_HEAD_COLS // tn
    return pl.pallas_call(
        functools.partial(_in_proj_kernel, n_head_tiles=n_head_tiles),
        grid=(t // tm, N_MAIN // tn),
        in_specs=[
            pl.BlockSpec((tm, D_MODEL), lambda i, j: (i, 0)),
            pl.BlockSpec((1, D_MODEL), lambda i, j: (0, 0)),
            pl.BlockSpec((D_MODEL, LANES), lambda i, j: (0, 0)),
            pl.BlockSpec((D_MODEL, tn), lambda i, j: (0, jnp.minimum(j, n_head_tiles - 1))),
            pl.BlockSpec((D_MODEL, tn), lambda i, j: (0, jnp.maximum(j - n_head_tiles, 0))),
        ],
        out_specs=[
            pl.BlockSpec((tm, tn), lambda i, j: (i, j)),
            pl.BlockSpec((tm, LANES), lambda i, j: (i, 0)),
        ],
        out_shape=[
            jax.ShapeDtypeStruct((t, N_MAIN), BF16),
            jax.ShapeDtypeStruct((t, LANES), BF16),
        ],
        scratch_shapes=[pltpu.VMEM((tm, D_MODEL), BF16)],
        compiler_params=pltpu.CompilerParams(
            dimension_semantics=("arbitrary", "arbitrary"),
            vmem_limit_bytes=VMEM_LIMIT),
        name="in_proj",
    )(h, g, w_z, w_head, w_tail)


def _cum_matrix(n, reverse):
    ri = lax.broadcasted_iota(jnp.int32, (n, n), 0)
    ci = lax.broadcasted_iota(jnp.int32, (n, n), 1)
    tri = jnp.where((ci >= ri) if reverse else (ci <= ri), 1.0, 0.0).astype(BF16)
    return jnp.concatenate([tri, tri], axis=1)


def _log_decay(z, wa, ba):
    lin = jnp.dot(z, wa, preferred_element_type=F32) + ba
    return (jnp.minimum(lin, 0.0) - jnp.log1p(jnp.exp(-jnp.abs(lin)))) / TAU


def _cum_decay(la, cum):
    la_hi = la.astype(BF16)
    la_lo = (la - la_hi.astype(F32)).astype(BF16)
    return jnp.dot(cum, jnp.concatenate([la_hi, la_lo], axis=0), preferred_element_type=F32)


def _gla_front_kernel(k_ref, v_ref, z_ref, wa_ref, ba_ref, s_ref):
    la = _log_decay(z_ref[...], wa_ref[...], ba_ref[...])
    row_id = lax.broadcasted_iota(jnp.int32, la.shape, 0)
    la = jnp.where(row_id >= FRONT, la, 0.0)
    b = _cum_decay(la, _cum_matrix(CHUNK, False))
    k_e = (k_ref[...].astype(F32) * jnp.exp(b[CHUNK - 1:CHUNK, :] - b)).astype(BF16)
    for h in range(HEADS):
        s_ref[h] = lax.dot_general(k_e[:, h * DK:(h + 1) * DK], v_ref[:, h * DV:(h + 1) * DV], _TN,
                                   preferred_element_type=F32)


def _gla_front(proj_front, z_front, wa, ba):
    return pl.pallas_call(
        _gla_front_kernel,
        grid=(1,),
        in_specs=[
            pl.BlockSpec((CHUNK, DQK), lambda i: (0, 1)),
            pl.BlockSpec((CHUNK, DV_TOT), lambda i: (0, 1)),
            pl.BlockSpec((CHUNK, LANES), lambda i: (0, 0)),
            pl.BlockSpec((LANES, DQK), lambda i: (0, 0)),
            pl.BlockSpec((1, DQK), lambda i: (0, 0)),
        ],
        out_specs=pl.BlockSpec((HEADS, DK, DV), lambda i: (0, 0, 0)),
        out_shape=jax.ShapeDtypeStruct((HEADS, DK, DV), F32),
        name="gla_front",
    )(proj_front, proj_front, z_front, wa, ba)


def _gla_kernel(blk_ref, first_ref, q_ref, k_ref, v_ref, z_ref, wa_ref, ba_ref, *rest, reverse):
    if reverse:
        of_ref, r_ref, gg_ref, o_ref, s_ref = rest
    else:
        s0_ref, o_ref, s_ref = rest
    t = pl.program_id(0)
    n = GLA_CHUNK
    chunks = range(GLA_BLOCK_CHUNKS)
    heads = range(HEADS)
    ks = [slice(h * DK, (h + 1) * DK) for h in heads]
    vs = [slice(h * DV, (h + 1) * DV) for h in heads]

    @pl.when(first_ref[t] == 1)
    def _():
        s_ref[...] = jnp.zeros_like(s_ref) if reverse else s0_ref[...]

    ri = lax.broadcasted_iota(jnp.int32, (n, n), 0)
    ci = lax.broadcasted_iota(jnp.int32, (n, n), 1)
    score_mask = (ci > ri) if reverse else (ci <= ri)
    cum = _cum_matrix(n, reverse)
    mid_row = n // 2 if reverse else n // 2 - 1
    tot_row = 0 if reverse else n - 1

    la = _log_decay(z_ref[...], wa_ref[...], ba_ref[...])
    b = [_cum_decay(la[c * n:(c + 1) * n, :], cum) for c in chunks]

    q_in, k_in, q_st, k_st, dec = [], [], [], [], []
    for c in chunks:
        sl = pl.ds(c * n, n)
        b_mid = b[c][mid_row:mid_row + 1, :]
        b_tot = b[c][tot_row:tot_row + 1, :]
        q = q_ref[sl, :].astype(F32) * (DK ** -0.5)
        k = k_ref[sl, :].astype(F32)
        q_in.append((q * jnp.exp(b[c] - b_mid)).astype(BF16))
        k_in.append((k * jnp.exp(b_mid - b[c])).astype(BF16))
        q_st.append((q * jnp.exp(b[c])).astype(BF16))
        k_st.append((k * jnp.exp(b_tot - b[c])).astype(BF16))
        dec.append(jnp.broadcast_to(jnp.exp(b_tot), (LANES, DQK)))

    v = [[v_ref[pl.ds(c * n, n), vs[h]] for h in heads] for c in chunks]
    scores = [[lax.dot_general(q_in[c][:, ks[h]], k_in[c][:, ks[h]], _NT, preferred_element_type=F32)
               for h in heads] for c in chunks]
    probs = [[jnp.where(score_mask, scores[c][h], 0.0).astype(BF16) for h in heads] for c in chunks]
    o_intra = [[jnp.dot(probs[c][h], v[c][h], preferred_element_type=F32) for h in heads] for c in chunks]
    upd = [[lax.dot_general(k_st[c][:, ks[h]], v[c][h], _TN, preferred_element_type=F32) for h in heads]
           for c in chunks]
    dec_t = [[dec[c][:, ks[h]].T for h in heads] for c in chunks]

    order = range(GLA_BLOCK_CHUNKS - 1, -1, -1) if reverse else chunks
    for c in order:
        sl = pl.ds(c * n, n)
        state = [s_ref[h] for h in heads]
        o_inter = [jnp.dot(q_st[c][:, ks[h]], state[h].astype(BF16), preferred_element_type=F32) for h in heads]
        for h in heads:
            s_ref[h] = state[h] * jnp.concatenate([dec_t[c][h]] * (DV // LANES), axis=1) + upd[c][h]
        for h in heads:
            o = o_intra[c][h] + o_inter[h]
            if reverse:
                o = _rms(o + of_ref[sl, vs[h]], gg_ref[:, vs[h]])
                r = r_ref[sl, vs[h]].astype(F32)
                o_ref[sl, vs[h]] = (o * (r * _sigmoid(r))).astype(o_ref.dtype)
            else:
                o_ref[sl, vs[h]] = o


def _gla(proj, z, wa, ba, tables, *, reverse, s_front=None, o_fwd=None, g_gla=None):
    t = proj.shape[0]
    blk, first = tables
    rows = lambda width, col: pl.BlockSpec((GLA_ROWS, width), lambda s, blk_ref, first_ref: (blk_ref[s], col))
    const = lambda shape: pl.BlockSpec(shape, lambda s, blk_ref, first_ref: (0,) * len(shape))
    in_specs = [rows(DQK, 0), rows(DQK, 1), rows(DV_TOT, 1), rows(LANES, 0), const((LANES, DQK)), const((1, DQK))]
    args = [proj, proj, proj, z, wa, ba]
    if reverse:
        in_specs += [rows(DV_TOT, 0), rows(DV_TOT, COL_R), const((1, DV_TOT))]
        args += [o_fwd, proj, g_gla]
    else:
        in_specs += [const((HEADS, DK, DV))]
        args += [s_front]
    return pl.pallas_call(
        functools.partial(_gla_kernel, reverse=reverse),
        grid_spec=pltpu.PrefetchScalarGridSpec(
            num_scalar_prefetch=2,
            grid=(blk.shape[0],),
            in_specs=in_specs,
            out_specs=rows(DV_TOT, 0),
            scratch_shapes=[pltpu.VMEM((HEADS, DK, DV), F32)],
        ),
        out_shape=jax.ShapeDtypeStruct((t, DV_TOT), BF16 if reverse else F32),
        compiler_params=pltpu.CompilerParams(
            dimension_semantics=("arbitrary",),
            vmem_limit_bytes=VMEM_LIMIT),
        name="gla_bwd" if reverse else "gla_fwd",
    )(blk, first, *args)


def _gla_tables(n_seq, seq_len, reverse):
    nb = seq_len // GLA_ROWS
    order = np.arange(nb - 1, -1, -1) if reverse else np.arange(nb)
    blk = np.concatenate([s * nb + order for s in range(n_seq)]).astype(np.int32)
    first = np.tile(np.arange(nb) == 0, n_seq).astype(np.int32)
    return jnp.asarray(blk), jnp.asarray(first)


def _mix_kernel(og_ref, ga_ref, gb_ref, h_ref, a_ref, gt_ref, ap_ref, gtp_ref, an_ref, gtn_ref, af_ref, gtf_ref,
                wgo_ref, wco_ref, wout_ref, wdw_ref, bdw_ref, lng_ref, lnb_ref, bco_ref, gpost_ref, gpre_ref,
                h1_ref, u2_ref, g_scr, y_scr, yc_scr, m_scr, *, tm, rb, tiles_per_seq, n_tiles):
    i = pl.program_id(0)
    pos = jnp.minimum(i, n_tiles - 1) % tiles_per_seq
    seq_start = pos == 0
    seq_end = pos == tiles_per_seq - 1

    @pl.when(i == 0)
    def _():
        yc_scr[...] = jnp.zeros_like(yc_scr)

    def glu(a, gt):
        return a.astype(F32) * _sigmoid(gt.astype(F32))

    g_prev = glu(jnp.where(seq_start, af_ref[...], ap_ref[...]), jnp.where(seq_start, gtf_ref[...], gtp_ref[...]))
    g_next = jnp.where(seq_end, 0.0, glu(an_ref[...], gtn_ref[...]))
    g_cur = glu(a_ref[...], gt_ref[...])
    for c in range(N_CBLK):
        cs = slice(c * LANES, (c + 1) * LANES)
        g_scr[c, 0:HALO, :] = g_prev[:, cs]
        g_scr[c, HALO:HALO + tm, :] = g_cur[:, cs]
        g_scr[c, HALO + tm:, :] = g_next[:, cs]

    def slab(s, carry):
        for cc in range(CBLK_PER_SLAB):
            c = s * CBLK_PER_SLAB + cc
            for r0 in range(0, tm, rb):
                acc = jnp.zeros((rb, LANES), F32)
                for w in range(CONV_W):
                    start = HALO - CONV_PAD + w + r0
                    acc = acc + g_scr[c, start:start + rb, :] * wdw_ref[c, w:w + 1, :]
                y_scr[c, r0:r0 + rb, :] = acc + bdw_ref[c]
        cols = pl.ds(pl.multiple_of(s * MXU_DIM, MXU_DIM), MXU_DIM)
        y_a = jnp.dot(og_ref[...], wgo_ref[s], preferred_element_type=F32)
        y_b = jnp.dot(yc_scr[...], wco_ref[s], preferred_element_type=F32) + bco_ref[s]
        merged = _sigmoid(ga_ref[:, cols].astype(F32)) * y_a + _sigmoid(gb_ref[:, cols].astype(F32)) * y_b
        m_scr[s] = merged.astype(BF16)
        return carry

    lax.fori_loop(0, N_SLAB, slab, 0)

    merged = jnp.concatenate([m_scr[s] for s in range(N_SLAB)], axis=1)
    mix = jnp.dot(merged, wout_ref[...], preferred_element_type=F32)
    h1 = h_ref[...] + _rms(mix, gpost_ref[...])
    h1_ref[...] = h1
    u2_ref[...] = _rms(h1, gpre_ref[...]).astype(BF16)

    s1 = jnp.zeros((tm, LANES), F32)
    for c in range(N_CBLK):
        s1 = s1 + y_scr[c]
    mu = jnp.sum(s1, axis=-1, keepdims=True) / D_MODEL
    s2 = jnp.zeros((tm, LANES), F32)
    for c in range(N_CBLK):
        d = y_scr[c] - mu
        s2 = s2 + d * d
    inv = lax.rsqrt(jnp.sum(s2, axis=-1, keepdims=True) / D_MODEL + EPS)
    for c in range(N_CBLK):
        yn = (y_scr[c] - mu) * inv * lng_ref[c] + lnb_ref[c]
        yc_scr[:, c * LANES:(c + 1) * LANES] = (yn * _sigmoid(yn)).astype(BF16)


def _mix(og, proj, proj_front, h, seq_len, wgo, wco, wout, wdw, bdw, lng, lnb, bco, gpost, gpre, tm, rb):
    t = h.shape[0]
    n_tiles = t // tm
    hb = tm // HALO
    last_hblk = t // HALO - 1
    conv_tile = lambda i: jnp.minimum(i, n_tiles - 1)
    proj_tile = lambda i: jnp.maximum(i - 1, 0)
    ptile = lambda col: pl.BlockSpec((tm, D_MODEL), lambda i: (proj_tile(i), col))
    ctile = lambda col: pl.BlockSpec((tm, D_MODEL), lambda i: (conv_tile(i), col))
    prev = lambda col: pl.BlockSpec((HALO, D_MODEL), lambda i: (jnp.maximum(conv_tile(i) * hb - 1, 0), col))
    nxt = lambda col: pl.BlockSpec((HALO, D_MODEL),
                                   lambda i: (jnp.minimum((conv_tile(i) + 1) * hb, last_hblk), col))
    meta = lambda col: pl.BlockSpec((HALO, D_MODEL), lambda i: (FRONT // HALO, col))
    resident = lambda shape: pl.BlockSpec(shape, lambda i: (0,) * len(shape), pipeline_mode=pl.Buffered(1))
    vec = resident((1, D_MODEL))
    cvec = resident((N_CBLK, 1, LANES))
    return pl.pallas_call(
        functools.partial(_mix_kernel, tm=tm, rb=rb, tiles_per_seq=seq_len // tm, n_tiles=n_tiles),
        grid=(n_tiles + 1,),
        in_specs=[
            ptile(0), ptile(COL_P + 2), ptile(COL_P + 3), ptile(0),
            ctile(COL_P), ctile(COL_P + 1),
            prev(COL_P), prev(COL_P + 1),
            nxt(COL_P), nxt(COL_P + 1),
            meta(COL_P), meta(COL_P + 1),
            resident((N_SLAB, DV_TOT, MXU_DIM)), resident((N_SLAB, D_MODEL, MXU_DIM)), resident((D_MODEL, D_MODEL)),
            resident((N_CBLK, CONV_W + 1, LANES)), cvec, cvec, cvec, resident((N_SLAB, 1, MXU_DIM)), vec, vec,
        ],
        out_specs=[ptile(0), ptile(0)],
        out_shape=[jax.ShapeDtypeStruct((t, D_MODEL), F32), jax.ShapeDtypeStruct((t, D_MODEL), BF16)],
        scratch_shapes=[
            pltpu.VMEM((N_CBLK, tm + 2 * HALO, LANES), F32),
            pltpu.VMEM((N_CBLK, tm, LANES), F32),
            pltpu.VMEM((tm, D_MODEL), BF16),
            pltpu.VMEM((N_SLAB, tm, MXU_DIM), BF16),
        ],
        compiler_params=pltpu.CompilerParams(
            dimension_semantics=("arbitrary",),
            vmem_limit_bytes=VMEM_LIMIT),
        name="mix",
    )(og, proj, proj, h, proj, proj, proj, proj, proj, proj, proj_front, proj_front,
      wgo, wco, wout, wdw, bdw, lng, lnb, bco, gpost, gpre)


def _mlp_kernel(u_ref, h_ref, wu_ref, wd_ref, g_ref, o_ref):
    f = pl.program_id(1)

    @pl.when(f == 0)
    def _():
        o_ref[...] = jnp.zeros_like(o_ref)

    up = jnp.dot(u_ref[...], wu_ref[...], preferred_element_type=F32)
    act = jnp.square(jnp.maximum(up, 0.0)).astype(BF16)
    o_ref[...] += jnp.dot(act, wd_ref[...], preferred_element_type=F32)

    @pl.when(f == pl.num_programs(1) - 1)
    def _():
        o_ref[...] = h_ref[...] + _rms(o_ref[...], g_ref[...])


def _mlp(u2, h1, w_up, w_down, g, tm, tf):
    t = u2.shape[0]
    return pl.pallas_call(
        _mlp_kernel,
        grid=(t // tm, D_FF // tf),
        in_specs=[
            pl.BlockSpec((tm, D_MODEL), lambda i, f: (i, 0)),
            pl.BlockSpec((tm, D_MODEL), lambda i, f: (i, 0)),
            pl.BlockSpec((D_MODEL, tf), lambda i, f: (0, f)),
            pl.BlockSpec((tf, D_MODEL), lambda i, f: (f, 0)),
            pl.BlockSpec((1, D_MODEL), lambda i, f: (0, 0)),
        ],
        out_specs=pl.BlockSpec((tm, D_MODEL), lambda i, f: (i, 0)),
        out_shape=jax.ShapeDtypeStruct((t, D_MODEL), F32),
        compiler_params=pltpu.CompilerParams(
            dimension_semantics=("arbitrary", "arbitrary"),
            vmem_limit_bytes=VMEM_LIMIT),
        name="mlp",
    )(u2, h1, w_up, w_down, g)


def _cblocks(vec):
    return vec.astype(F32).reshape(N_CBLK, 1, LANES)


def _col_slabs(w):
    return w.reshape(w.shape[0], N_SLAB, MXU_DIM).transpose(1, 0, 2).astype(BF16)


def kernel(x_prompt, x_sample, meta_tokens, g_pre_mix, w_in, w_a2_f, b_a_f, w_a2_b, b_a_b, g_gla,
           w_gla_o, w_dw, b_dw, ln_g, ln_b, w_conv_o, b_conv_o, w_out, g_post_mix, g_pre_mlp,
           w_up, w_down, g_post_mlp):
    assert w_in.shape[0] == 1
    l = 0
    tm_proj, tn_proj, tm_mix, rb_mix, tm_mlp, tf_mlp = 1024, 1024, 256, 64, 512, 1024
    assert N_HEAD_COLS % tn_proj == 0 and N_TAIL_COLS % tn_proj == 0
    row = lambda v: v.astype(F32).reshape(1, -1)
    z0 = N_HEAD_COLS

    w_head = w_in[l][:, :z0].astype(BF16)
    w_tail = w_in[l][:, z0 + 2 * GATE_RANK:].astype(BF16)
    w_z = jnp.pad(w_in[l][:, z0:z0 + 2 * GATE_RANK], ((0, 0), (0, LANES - 2 * GATE_RANK))).astype(BF16)
    wa_f = jnp.pad(w_a2_f[l], ((0, LANES - GATE_RANK), (0, 0))).astype(BF16)
    wa_b = jnp.pad(w_a2_b[l], ((GATE_RANK, LANES - 2 * GATE_RANK), (0, 0))).astype(BF16)
    wdw = jnp.pad(w_dw[l].astype(F32), ((0, 1), (0, 0))).reshape(CONV_W + 1, N_CBLK, LANES).transpose(1, 0, 2)
    w_go, w_co, w_o = _col_slabs(w_gla_o[l]), _col_slabs(w_conv_o[l]), w_out[l].astype(BF16)
    b_co = b_conv_o[l].astype(F32).reshape(N_SLAB, 1, MXU_DIM)
    w_u, w_d = w_up[l].astype(BF16), w_down[l].astype(BF16)

    front = jnp.concatenate([jnp.zeros((FRONT, D_MODEL), F32), meta_tokens.astype(F32)], axis=0)
    proj_front, z_front = _in_proj(front, row(g_pre_mix[l]), w_head, w_tail, w_z, CHUNK, tn_proj)
    s_front = _gla_front(proj_front, z_front, wa_f, row(b_a_f[l]))

    outs = []
    for x in (x_prompt, x_sample):
        bsz, seq_len, _ = x.shape
        assert seq_len % GLA_ROWS == 0 and seq_len % tm_mix == 0
        h = x.reshape(bsz * seq_len, D_MODEL)
        proj, z = _in_proj(h, row(g_pre_mix[l]), w_head, w_tail, w_z, tm_proj, tn_proj)
        o_f = _gla(proj, z, wa_f, row(b_a_f[l]), _gla_tables(bsz, seq_len, False), reverse=False, s_front=s_front)
        og = _gla(proj, z, wa_b, row(b_a_b[l]), _gla_tables(bsz, seq_len, True), reverse=True,
                  o_fwd=o_f, g_gla=row(g_gla[l]))
        h1, u2 = _mix(og, proj, proj_front, h, seq_len, w_go, w_co, w_o, wdw, _cblocks(b_dw[l]),
                      _cblocks(ln_g[l]), _cblocks(ln_b[l]), b_co, row(g_post_mix[l]),
                      row(g_pre_mlp[l]), tm_mix, rb_mix)
        y = _mlp(u2, h1, w_u, w_d, row(g_post_mlp[l]), tm_mlp, tf_mlp)
        outs.append(y.reshape(bsz, seq_len, D_MODEL))
    return tuple(outs)
```

```python
import functools

import jax
import jax.numpy as jnp
import numpy as np
from jax import lax
from jax.experimental import pallas as pl
from jax.experimental.pallas import tpu as pltpu

F32 = jnp.float32
BF16 = jnp.bfloat16

D_MODEL = 2048
N_META = 16
HEADS = 4
DK = 256
DV = 512
DQK = HEADS * DK
DV_TOT = HEADS * DV
GATE_RANK = 16
TAU = 16.0
CHUNK = 64
FRONT = CHUNK - N_META
CONV_W = 31
CONV_PAD = CONV_W // 2
D_FF = 4 * D_MODEL
EPS = 1e-6

LANES = 128
MXU_DIM = 256
HALO = 16
N_CBLK = D_MODEL // LANES
N_SLAB = D_MODEL // MXU_DIM
CBLK_PER_SLAB = MXU_DIM // LANES
N_HEAD_COLS = 2 * DQK + 2 * DV_TOT
N_MAIN = N_HEAD_COLS + 4 * D_MODEL
N_PROJ = N_HEAD_COLS + 2 * D_MODEL
COL_R = (2 * DQK + DV_TOT) // D_MODEL
COL_GATE = N_HEAD_COLS // D_MODEL
PIECE = 512

GLA_CHUNK = 2 * CHUNK
GLA_BLOCK_CHUNKS = 2
GLA_ROWS = GLA_BLOCK_CHUNKS * GLA_CHUNK

VMEM_LIMIT = 58 * 1024 * 1024

_NT = (((1,), (1,)), ((), ()))
_TN = (((0,), (0,)), ((), ()))


def _sigmoid(x):
    return 1.0 / (1.0 + jnp.exp(-x))


def _rms(x, g):
    ms = jnp.mean(x * x, axis=-1, keepdims=True)
    return x * lax.rsqrt(ms + EPS) * g


def _pack_w_in_kernel(src_ref, shifted_ref, main_ref, extra_ref, o_ref):
    p = pl.program_id(0)

    @pl.when(shifted_ref[p] == 0)
    def _():
        o_ref[...] = main_ref[...].astype(BF16)

    @pl.when(shifted_ref[p] != 0)
    def _():
        wide = jnp.concatenate([main_ref[...], extra_ref[...]], axis=1)
        o_ref[...] = wide[:, 2 * GATE_RANK:2 * GATE_RANK + PIECE].astype(BF16)


def _pack_w_in(w, glu_width):
    n_glu = D_MODEL // PIECE
    per_group = glu_width // PIECE
    tail = N_HEAD_COLS // PIECE
    src = list(range(tail))
    for c in range(0, n_glu, per_group):
        src += [tail + c + d for d in range(per_group)] + [tail + n_glu + c + d for d in range(per_group)]
    src += [tail + 2 * n_glu + c for c in range(2 * n_glu)]
    shifted = [0] * tail + [1] * (len(src) - tail)
    lanes_per_piece = PIECE // LANES
    return pl.pallas_call(
        _pack_w_in_kernel,
        grid_spec=pltpu.PrefetchScalarGridSpec(
            num_scalar_prefetch=2,
            grid=(len(src),),
            in_specs=[
                pl.BlockSpec((D_MODEL, PIECE), lambda p, src_ref, sh_ref: (0, src_ref[p])),
                pl.BlockSpec((D_MODEL, LANES), lambda p, src_ref, sh_ref: (0, (src_ref[p] + 1) * lanes_per_piece)),
            ],
            out_specs=pl.BlockSpec((D_MODEL, PIECE), lambda p, src_ref, sh_ref: (0, p)),
        ),
        out_shape=jax.ShapeDtypeStruct((D_MODEL, N_MAIN), BF16),
        compiler_params=pltpu.CompilerParams(dimension_semantics=("arbitrary",), vmem_limit_bytes=VMEM_LIMIT),
        name="pack_w_in",
    )(jnp.asarray(np.array(src, np.int32)), jnp.asarray(np.array(shifted, np.int32)), w, w)


def _in_proj_kernel(x_ref, g_ref, wz_ref, w_ref, o_ref, glu_ref, z_ref, u_ref, *, tiles):
    n_plain, n_swish, n_glu = tiles
    j = pl.program_id(1)

    @pl.when(j == 0)
    def _():
        u = _rms(x_ref[...], g_ref[...]).astype(BF16)
        u_ref[...] = u
        z_ref[...] = jnp.dot(u, wz_ref[...], preferred_element_type=F32).astype(BF16)

    is_glu = jnp.logical_and(j >= n_plain + n_swish, j < n_plain + n_swish + n_glu)

    @pl.when(jnp.logical_not(is_glu))
    def _():
        acc = jnp.dot(u_ref[...], w_ref[...], preferred_element_type=F32)
        sig = _sigmoid(acc)
        out = jnp.where(j < n_plain, acc, jnp.where(j < n_plain + n_swish, acc * sig, sig))
        o_ref[...] = out.astype(BF16)

    @pl.when(is_glu)
    def _():
        acc = jnp.dot(u_ref[...], w_ref[...], preferred_element_type=F32)
        half = acc.shape[1] // 2
        glu_ref[...] = (acc[:, :half] * _sigmoid(acc[:, half:])).astype(BF16)


def _in_proj(h, g, w_main, w_z, tm, tn):
    t = h.shape[0]
    assert (2 * DQK + DV_TOT) % tn == 0 and DV_TOT % tn == 0
    n_plain, n_swish, n_glu = (2 * DQK + DV_TOT) // tn, DV_TOT // tn, 2 * D_MODEL // tn
    first_glu = n_plain + n_swish
    proj_col = lambda j: jnp.where(j < first_glu, j, jnp.maximum(j - n_glu, first_glu - 1))
    glu_col = lambda j: jnp.clip(j - first_glu, 0, n_glu - 1)
    return pl.pallas_call(
        functools.partial(_in_proj_kernel, tiles=(n_plain, n_swish, n_glu)),
        grid=(t // tm, N_MAIN // tn),
        in_specs=[
            pl.BlockSpec((tm, D_MODEL), lambda i, j: (i, 0)),
            pl.BlockSpec((1, D_MODEL), lambda i, j: (0, 0)),
            pl.BlockSpec((D_MODEL, LANES), lambda i, j: (0, 0)),
            pl.BlockSpec((D_MODEL, tn), lambda i, j: (0, j)),
        ],
        out_specs=[
            pl.BlockSpec((tm, tn), lambda i, j: (i, proj_col(j))),
            pl.BlockSpec((tm, tn // 2), lambda i, j: (i, glu_col(j))),
            pl.BlockSpec((tm, LANES), lambda i, j: (i, 0)),
        ],
        out_shape=[
            jax.ShapeDtypeStruct((t, N_PROJ), BF16),
            jax.ShapeDtypeStruct((t, D_MODEL), BF16),
            jax.ShapeDtypeStruct((t, LANES), BF16),
        ],
        scratch_shapes=[pltpu.VMEM((tm, D_MODEL), BF16)],
        compiler_params=pltpu.CompilerParams(
            dimension_semantics=("arbitrary", "arbitrary"),
            vmem_limit_bytes=VMEM_LIMIT),
        name="in_proj",
    )(h, g, w_z, w_main)


def _cum_matrix(n, reverse):
    ri = lax.broadcasted_iota(jnp.int32, (n, n), 0)
    ci = lax.broadcasted_iota(jnp.int32, (n, n), 1)
    tri = jnp.where((ci >= ri) if reverse else (ci <= ri), 1.0, 0.0).astype(BF16)
    return jnp.concatenate([tri, tri], axis=1)


def _log_decay(z, wa, ba):
    lin = jnp.dot(z, wa, preferred_element_type=F32) + ba
    return (jnp.minimum(lin, 0.0) - jnp.log1p(jnp.exp(-jnp.abs(lin)))) / TAU


def _cum_decay(la, cum):
    la_hi = la.astype(BF16)
    la_lo = (la - la_hi.astype(F32)).astype(BF16)
    return jnp.dot(cum, jnp.concatenate([la_hi, la_lo], axis=0), preferred_element_type=F32)


def _gla_front_kernel(k_ref, v_ref, z_ref, wa_ref, ba_ref, s_ref):
    la = _log_decay(z_ref[...], wa_ref[...], ba_ref[...])
    row_id = lax.broadcasted_iota(jnp.int32, la.shape, 0)
    la = jnp.where(row_id >= FRONT, la, 0.0)
    b = _cum_decay(la, _cum_matrix(CHUNK, False))
    k_e = (k_ref[...].astype(F32) * jnp.exp(b[CHUNK - 1:CHUNK, :] - b)).astype(BF16)
    for h in range(HEADS):
        s_ref[h] = lax.dot_general(k_e[:, h * DK:(h + 1) * DK], v_ref[:, h * DV:(h + 1) * DV], _TN,
                                   preferred_element_type=F32)


def _gla_front(proj_front, z_front, wa, ba):
    return pl.pallas_call(
        _gla_front_kernel,
        grid=(1,),
        in_specs=[
            pl.BlockSpec((CHUNK, DQK), lambda i: (0, 1)),
            pl.BlockSpec((CHUNK, DV_TOT), lambda i: (0, 1)),
            pl.BlockSpec((CHUNK, LANES), lambda i: (0, 0)),
            pl.BlockSpec((LANES, DQK), lambda i: (0, 0)),
            pl.BlockSpec((1, DQK), lambda i: (0, 0)),
        ],
        out_specs=pl.BlockSpec((HEADS, DK, DV), lambda i: (0, 0, 0)),
        out_shape=jax.ShapeDtypeStruct((HEADS, DK, DV), F32),
        name="gla_front",
    )(proj_front, proj_front, z_front, wa, ba)


def _gla_kernel(blk_ref, first_ref, q_ref, k_ref, v_ref, z_ref, wa_ref, ba_ref, *rest, reverse):
    if reverse:
        of_ref, r_ref, gg_ref, o_ref, s_ref = rest
    else:
        s0_ref, o_ref, s_ref = rest
    t = pl.program_id(0)
    n = GLA_CHUNK
    chunks = range(GLA_BLOCK_CHUNKS)
    heads = range(HEADS)
    ks = [slice(h * DK, (h + 1) * DK) for h in heads]
    vs = [slice(h * DV, (h + 1) * DV) for h in heads]

    @pl.when(first_ref[t] == 1)
    def _():
        s_ref[...] = jnp.zeros_like(s_ref) if reverse else s0_ref[...]

    ri = lax.broadcasted_iota(jnp.int32, (n, n), 0)
    ci = lax.broadcasted_iota(jnp.int32, (n, n), 1)
    score_mask = (ci > ri) if reverse else (ci <= ri)
    cum = _cum_matrix(n, reverse)
    mid_row = n // 2 if reverse else n // 2 - 1
    tot_row = 0 if reverse else n - 1

    la = _log_decay(z_ref[...], wa_ref[...], ba_ref[...])
    b = [_cum_decay(la[c * n:(c + 1) * n, :], cum) for c in chunks]

    q_in, k_in, q_st, k_st, dec = [], [], [], [], []
    for c in chunks:
        sl = pl.ds(c * n, n)
        b_mid = b[c][mid_row:mid_row + 1, :]
        b_tot = b[c][tot_row:tot_row + 1, :]
        q = q_ref[sl, :].astype(F32) * (DK ** -0.5)
        k = k_ref[sl, :].astype(F32)
        q_in.append((q * jnp.exp(b[c] - b_mid)).astype(BF16))
        k_in.append((k * jnp.exp(b_mid - b[c])).astype(BF16))
        q_st.append((q * jnp.exp(b[c])).astype(BF16))
        k_st.append((k * jnp.exp(b_tot - b[c])).astype(BF16))
        dec.append(jnp.broadcast_to(jnp.exp(b_tot), (LANES, DQK)))

    v = [[v_ref[pl.ds(c * n, n), vs[h]] for h in heads] for c in chunks]
    scores = [[lax.dot_general(q_in[c][:, ks[h]], k_in[c][:, ks[h]], _NT, preferred_element_type=F32)
               for h in heads] for c in chunks]
    probs = [[jnp.where(score_mask, scores[c][h], 0.0).astype(BF16) for h in heads] for c in chunks]
    o_intra = [[jnp.dot(probs[c][h], v[c][h], preferred_element_type=F32) for h in heads] for c in chunks]
    upd = [[lax.dot_general(k_st[c][:, ks[h]], v[c][h], _TN, preferred_element_type=F32) for h in heads]
           for c in chunks]
    dec_t = [[dec[c][:, ks[h]].T for h in heads] for c in chunks]

    order = range(GLA_BLOCK_CHUNKS - 1, -1, -1) if reverse else chunks
    for c in order:
        sl = pl.ds(c * n, n)
        state = [s_ref[h] for h in heads]
        o_inter = [jnp.dot(q_st[c][:, ks[h]], state[h].astype(BF16), preferred_element_type=F32) for h in heads]
        for h in heads:
            s_ref[h] = state[h] * jnp.concatenate([dec_t[c][h]] * (DV // LANES), axis=1) + upd[c][h]
        for h in heads:
            o = o_intra[c][h] + o_inter[h]
            if reverse:
                o = _rms(o + of_ref[sl, vs[h]], gg_ref[:, vs[h]])
                o_ref[sl, vs[h]] = (o * r_ref[sl, vs[h]].astype(F32)).astype(o_ref.dtype)
            else:
                o_ref[sl, vs[h]] = o


def _gla(proj, z, wa, ba, tables, *, reverse, s_front=None, o_fwd=None, g_gla=None):
    t = proj.shape[0]
    blk, first = tables
    rows = lambda width, col: pl.BlockSpec((GLA_ROWS, width), lambda s, blk_ref, first_ref: (blk_ref[s], col))
    const = lambda shape: pl.BlockSpec(shape, lambda s, blk_ref, first_ref: (0,) * len(shape))
    in_specs = [rows(DQK, 0), rows(DQK, 1), rows(DV_TOT, 1), rows(LANES, 0), const((LANES, DQK)), const((1, DQK))]
    args = [proj, proj, proj, z, wa, ba]
    if reverse:
        in_specs += [rows(DV_TOT, 0), rows(DV_TOT, COL_R), const((1, DV_TOT))]
        args += [o_fwd, proj, g_gla]
    else:
        in_specs += [const((HEADS, DK, DV))]
        args += [s_front]
    return pl.pallas_call(
        functools.partial(_gla_kernel, reverse=reverse),
        grid_spec=pltpu.PrefetchScalarGridSpec(
            num_scalar_prefetch=2,
            grid=(blk.shape[0],),
            in_specs=in_specs,
            out_specs=rows(DV_TOT, 0),
            scratch_shapes=[pltpu.VMEM((HEADS, DK, DV), F32)],
        ),
        out_shape=jax.ShapeDtypeStruct((t, DV_TOT), BF16 if reverse else F32),
        compiler_params=pltpu.CompilerParams(
            dimension_semantics=("arbitrary",),
            vmem_limit_bytes=VMEM_LIMIT),
        name="gla_bwd" if reverse else "gla_fwd",
    )(blk, first, *args)


def _gla_tables(n_seq, seq_len, reverse):
    nb = seq_len // GLA_ROWS
    order = np.arange(nb - 1, -1, -1) if reverse else np.arange(nb)
    blk = np.concatenate([s * nb + order for s in range(n_seq)]).astype(np.int32)
    first = np.tile(np.arange(nb) == 0, n_seq).astype(np.int32)
    return jnp.asarray(blk), jnp.asarray(first)


def _mix_kernel(og_ref, ga_ref, gb_ref, h_ref, g_ref, gp_ref, gn_ref, gf_ref,
                wgo_ref, wco_ref, wout_ref, wdw_ref, bdw_ref, lng_ref, lnb_ref, bco_ref, gpost_ref, gpre_ref,
                h1_ref, u2_ref, g_scr, y_scr, yc_scr, m_scr, *, tm, rb, tiles_per_seq, n_tiles):
    i = pl.program_id(0)
    pos = jnp.minimum(i, n_tiles - 1) % tiles_per_seq
    seq_start = pos == 0
    seq_end = pos == tiles_per_seq - 1

    @pl.when(i == 0)
    def _():
        yc_scr[...] = jnp.zeros_like(yc_scr)

    g_prev = jnp.where(seq_start, gf_ref[...], gp_ref[...]).astype(F32)
    g_next = jnp.where(seq_end, 0.0, gn_ref[...].astype(F32))
    g_cur = g_ref[...].astype(F32)
    for c in range(N_CBLK):
        cs = slice(c * LANES, (c + 1) * LANES)
        g_scr[c, 0:HALO, :] = g_prev[:, cs]
        g_scr[c, HALO:HALO + tm, :] = g_cur[:, cs]
        g_scr[c, HALO + tm:, :] = g_next[:, cs]

    def slab(s, carry):
        for cc in range(CBLK_PER_SLAB):
            c = s * CBLK_PER_SLAB + cc
            for r0 in range(0, tm, rb):
                acc = jnp.zeros((rb, LANES), F32)
                for w in range(CONV_W):
                    start = HALO - CONV_PAD + w + r0
                    acc = acc + g_scr[c, start:start + rb, :] * wdw_ref[c, w:w + 1, :]
                y_scr[c, r0:r0 + rb, :] = acc + bdw_ref[c]
        cols = pl.ds(pl.multiple_of(s * MXU_DIM, MXU_DIM), MXU_DIM)
        y_a = jnp.dot(og_ref[...], wgo_ref[s], preferred_element_type=F32)
        y_b = jnp.dot(yc_scr[...], wco_ref[s], preferred_element_type=F32) + bco_ref[s]
        merged = ga_ref[:, cols].astype(F32) * y_a + gb_ref[:, cols].astype(F32) * y_b
        m_scr[s] = merged.astype(BF16)
        return carry

    lax.fori_loop(0, N_SLAB, slab, 0)

    merged = jnp.concatenate([m_scr[s] for s in range(N_SLAB)], axis=1)
    mix = jnp.dot(merged, wout_ref[...], preferred_element_type=F32)
    h1 = h_ref[...] + _rms(mix, gpost_ref[...])
    h1_ref[...] = h1
    u2_ref[...] = _rms(h1, gpre_ref[...]).astype(BF16)

    s1 = jnp.zeros((tm, LANES), F32)
    for c in range(N_CBLK):
        s1 = s1 + y_scr[c]
    mu = jnp.sum(s1, axis=-1, keepdims=True) / D_MODEL
    s2 = jnp.zeros((tm, LANES), F32)
    for c in range(N_CBLK):
        d = y_scr[c] - mu
        s2 = s2 + d * d
    inv = lax.rsqrt(jnp.sum(s2, axis=-1, keepdims=True) / D_MODEL + EPS)
    for c in range(N_CBLK):
        yn = (y_scr[c] - mu) * inv * lng_ref[c] + lnb_ref[c]
        yc_scr[:, c * LANES:(c + 1) * LANES] = (yn * _sigmoid(yn)).astype(BF16)


def _mix(og, proj, glu, glu_front, h, seq_len, wgo, wco, wout, wdw, bdw, lng, lnb, bco, gpost, gpre, tm, rb):
    t = h.shape[0]
    n_tiles = t // tm
    hb = tm // HALO
    last_hblk = t // HALO - 1
    conv_tile = lambda i: jnp.minimum(i, n_tiles - 1)
    proj_tile = lambda i: jnp.maximum(i - 1, 0)
    ptile = lambda col: pl.BlockSpec((tm, D_MODEL), lambda i: (proj_tile(i), col))
    ctile = pl.BlockSpec((tm, D_MODEL), lambda i: (conv_tile(i), 0))
    prev = pl.BlockSpec((HALO, D_MODEL), lambda i: (jnp.maximum(conv_tile(i) * hb - 1, 0), 0))
    nxt = pl.BlockSpec((HALO, D_MODEL), lambda i: (jnp.minimum((conv_tile(i) + 1) * hb, last_hblk), 0))
    meta = pl.BlockSpec((HALO, D_MODEL), lambda i: (FRONT // HALO, 0))
    resident = lambda shape: pl.BlockSpec(shape, lambda i: (0,) * len(shape), pipeline_mode=pl.Buffered(1))
    vec = resident((1, D_MODEL))
    cvec = resident((N_CBLK, 1, LANES))
    return pl.pallas_call(
        functools.partial(_mix_kernel, tm=tm, rb=rb, tiles_per_seq=seq_len // tm, n_tiles=n_tiles),
        grid=(n_tiles + 1,),
        in_specs=[
            ptile(0), ptile(COL_GATE), ptile(COL_GATE + 1), ptile(0),
            ctile, prev, nxt, meta,
            resident((N_SLAB, DV_TOT, MXU_DIM)), resident((N_SLAB, D_MODEL, MXU_DIM)), resident((D_MODEL, D_MODEL)),
            resident((N_CBLK, CONV_W + 1, LANES)), cvec, cvec, cvec, resident((N_SLAB, 1, MXU_DIM)), vec, vec,
        ],
        out_specs=[ptile(0), ptile(0)],
        out_shape=[jax.ShapeDtypeStruct((t, D_MODEL), F32), jax.ShapeDtypeStruct((t, D_MODEL), BF16)],
        scratch_shapes=[
            pltpu.VMEM((N_CBLK, tm + 2 * HALO, LANES), F32),
            pltpu.VMEM((N_CBLK, tm, LANES), F32),
            pltpu.VMEM((tm, D_MODEL), BF16),
            pltpu.VMEM((N_SLAB, tm, MXU_DIM), BF16),
        ],
        compiler_params=pltpu.CompilerParams(
            dimension_semantics=("arbitrary",),
            vmem_limit_bytes=VMEM_LIMIT),
        name="mix",
    )(og, proj, proj, h, glu, glu, glu, glu_front,
      wgo, wco, wout, wdw, bdw, lng, lnb, bco, gpost, gpre)


def _mlp_kernel(u_ref, h_ref, wu_ref, wd_ref, g_ref, o_ref):
    f = pl.program_id(1)

    @pl.when(f == 0)
    def _():
        o_ref[...] = jnp.zeros_like(o_ref)

    up = jnp.dot(u_ref[...], wu_ref[...], preferred_element_type=F32)
    act = jnp.square(jnp.maximum(up, 0.0)).astype(BF16)
    o_ref[...] += jnp.dot(act, wd_ref[...], preferred_element_type=F32)

    @pl.when(f == pl.num_programs(1) - 1)
    def _():
        o_ref[...] = h_ref[...] + _rms(o_ref[...], g_ref[...])


def _mlp(u2, h1, w_up, w_down, g, tm, tf):
    t = u2.shape[0]
    return pl.pallas_call(
        _mlp_kernel,
        grid=(t // tm, D_FF // tf),
        in_specs=[
            pl.BlockSpec((tm, D_MODEL), lambda i, f: (i, 0)),
            pl.BlockSpec((tm, D_MODEL), lambda i, f: (i, 0)),
            pl.BlockSpec((D_MODEL, tf), lambda i, f: (0, f)),
            pl.BlockSpec((tf, D_MODEL), lambda i, f: (f, 0)),
            pl.BlockSpec((1, D_MODEL), lambda i, f: (0, 0)),
        ],
        out_specs=pl.BlockSpec((tm, D_MODEL), lambda i, f: (i, 0)),
        out_shape=jax.ShapeDtypeStruct((t, D_MODEL), F32),
        compiler_params=pltpu.CompilerParams(
            dimension_semantics=("arbitrary", "arbitrary"),
            vmem_limit_bytes=VMEM_LIMIT),
        name="mlp",
    )(u2, h1, w_up, w_down, g)


def _cblocks(vec):
    return vec.astype(F32).reshape(N_CBLK, 1, LANES)


def _col_slabs(w):
    return w.reshape(w.shape[0], N_SLAB, MXU_DIM).transpose(1, 0, 2).astype(BF16)


def kernel(x_prompt, x_sample, meta_tokens, g_pre_mix, w_in, w_a2_f, b_a_f, w_a2_b, b_a_b, g_gla,
           w_gla_o, w_dw, b_dw, ln_g, ln_b, w_conv_o, b_conv_o, w_out, g_post_mix, g_pre_mlp,
           w_up, w_down, g_post_mlp):
    assert w_in.shape[0] == 1
    l = 0
    tm_proj, tn_proj, tm_mix, rb_mix, tm_mlp, tf_mlp = 1024, 2048, 256, 64, 512, 1024
    row = lambda v: v.astype(F32).reshape(1, -1)
    z0 = N_HEAD_COLS

    w_main = _pack_w_in(w_in[l], tn_proj // 2)
    w_z = jnp.pad(w_in[l][:, z0:z0 + 2 * GATE_RANK], ((0, 0), (0, LANES - 2 * GATE_RANK))).astype(BF16)
    wa_f = jnp.pad(w_a2_f[l], ((0, LANES - GATE_RANK), (0, 0))).astype(BF16)
    wa_b = jnp.pad(w_a2_b[l], ((GATE_RANK, LANES - 2 * GATE_RANK), (0, 0))).astype(BF16)
    wdw = jnp.pad(w_dw[l].astype(F32), ((0, 1), (0, 0))).reshape(CONV_W + 1, N_CBLK, LANES).transpose(1, 0, 2)
    w_go, w_co, w_o = _col_slabs(w_gla_o[l]), _col_slabs(w_conv_o[l]), w_out[l].astype(BF16)
    b_co = b_conv_o[l].astype(F32).reshape(N_SLAB, 1, MXU_DIM)
    w_u, w_d = w_up[l].astype(BF16), w_down[l].astype(BF16)

    front = jnp.concatenate([jnp.zeros((FRONT, D_MODEL), F32), meta_tokens.astype(F32)], axis=0)
    proj_front, glu_front, z_front = _in_proj(front, row(g_pre_mix[l]), w_main, w_z, CHUNK, tn_proj)
    s_front = _gla_front(proj_front, z_front, wa_f, row(b_a_f[l]))

    outs = []
    for x in (x_prompt, x_sample):
        bsz, seq_len, _ = x.shape
        assert seq_len % GLA_ROWS == 0 and seq_len % tm_mix == 0
        h = x.reshape(bsz * seq_len, D_MODEL)
        proj, glu, z = _in_proj(h, row(g_pre_mix[l]), w_main, w_z, tm_proj, tn_proj)
        o_f = _gla(proj, z, wa_f, row(b_a_f[l]), _gla_tables(bsz, seq_len, False), reverse=False, s_front=s_front)
        og = _gla(proj, z, wa_b, row(b_a_b[l]), _gla_tables(bsz, seq_len, True), reverse=True,
                  o_fwd=o_f, g_gla=row(g_gla[l]))
        h1, u2 = _mix(og, proj, glu, glu_front, h, seq_len, w_go, w_co, w_o, wdw, _cblocks(b_dw[l]),
                      _cblocks(ln_g[l]), _cblocks(ln_b[l]), b_co, row(g_post_mix[l]),
                      row(g_pre_mlp[l]), tm_mix, rb_mix)
        y = _mlp(u2, h1, w_u, w_d, row(g_post_mlp[l]), tm_mlp, tf_mlp)
        outs.append(y.reshape(bsz, seq_len, D_MODEL))
    return tuple(outs)
```

```python
import functools

import jax
import jax.numpy as jnp
import numpy as np
from jax import lax
from jax.experimental import pallas as pl
from jax.experimental.pallas import tpu as pltpu

F32 = jnp.float32
BF16 = jnp.bfloat16

D_MODEL = 2048
N_META = 16
HEADS = 4
DK = 256
DV = 512
DQK = HEADS * DK
DV_TOT = HEADS * DV
GATE_RANK = 16
TAU = 16.0
CHUNK = 64
FRONT = CHUNK - N_META
CONV_W = 31
CONV_PAD = CONV_W // 2
D_FF = 4 * D_MODEL
EPS = 1e-6

LANES = 128
MXU_DIM = 256
HALO = 16
N_CBLK = D_MODEL // LANES
N_SLAB = D_MODEL // MXU_DIM
CBLK_PER_SLAB = MXU_DIM // LANES
N_HEAD_COLS = 2 * DQK + 2 * DV_TOT
N_MAIN = N_HEAD_COLS + 4 * D_MODEL
N_PROJ = N_HEAD_COLS + 2 * D_MODEL
COL_R = (2 * DQK + DV_TOT) // D_MODEL
COL_GATE = N_HEAD_COLS // D_MODEL
ROW_UNIT = 2 * GATE_RANK

GLA_CHUNK = 2 * CHUNK
GLA_BLOCK_CHUNKS = 2
GLA_ROWS = GLA_BLOCK_CHUNKS * GLA_CHUNK

VMEM_LIMIT = 58 * 1024 * 1024

_NT = (((1,), (1,)), ((), ()))
_TN = (((0,), (0,)), ((), ()))


def _sigmoid(x):
    return 1.0 / (1.0 + jnp.exp(-x))


def _rms(x, g):
    ms = jnp.mean(x * x, axis=-1, keepdims=True)
    return x * lax.rsqrt(ms + EPS) * g


def _in_proj_kernel(lo_ref, hi_ref, x_ref, g_ref, wz_ref, wlo_ref, whi_ref, o_ref, glu_ref, z_ref, u_ref, *, tiles):
    n_plain, n_swish, n_glu = tiles
    j = pl.program_id(1)
    half = wlo_ref.shape[0]

    @pl.when(j == 0)
    def _():
        u = _rms(x_ref[...], g_ref[...]).astype(BF16)
        u_ref[...] = u
        z_ref[...] = lax.dot_general(u, wz_ref[...], _NT, preferred_element_type=F32).astype(BF16)

    is_glu = jnp.logical_and(j >= n_plain + n_swish, j < n_plain + n_swish + n_glu)

    @pl.when(jnp.logical_not(is_glu))
    def _():
        for w_ref, cols in ((wlo_ref, slice(0, half)), (whi_ref, slice(half, 2 * half))):
            acc = lax.dot_general(u_ref[...], w_ref[...], _NT, preferred_element_type=F32)
            sig = _sigmoid(acc)
            out = jnp.where(j < n_plain, acc, jnp.where(j < n_plain + n_swish, acc * sig, sig))
            o_ref[:, cols] = out.astype(BF16)

    @pl.when(is_glu)
    def _():
        val = lax.dot_general(u_ref[...], wlo_ref[...], _NT, preferred_element_type=F32)
        gate = lax.dot_general(u_ref[...], whi_ref[...], _NT, preferred_element_type=F32)
        glu_ref[...] = (val * _sigmoid(gate)).astype(BF16)


def _in_proj(h, g, wt, w_z, tm, tn):
    t = h.shape[0]
    half = tn // 2
    assert (2 * DQK + DV_TOT) % tn == 0 and DV_TOT % tn == 0
    n_plain, n_swish, n_glu = (2 * DQK + DV_TOT) // tn, DV_TOT // tn, 2 * D_MODEL // tn
    first_glu = n_plain + n_swish
    z1 = N_HEAD_COLS + 2 * GATE_RANK
    lo = [j * tn for j in range(first_glu)] + [z1 + c * half for c in range(n_glu)]
    hi = [j * tn + half for j in range(first_glu)] + [z1 + D_MODEL + c * half for c in range(n_glu)]
    for s in range(z1 + 2 * D_MODEL, wt.shape[0], tn):
        lo.append(s)
        hi.append(s + half)
    n_tiles = len(lo)
    assert n_tiles * tn == N_MAIN and all(s % ROW_UNIT == 0 for s in lo + hi)
    proj_col = lambda j: jnp.where(j < first_glu, j, jnp.maximum(j - n_glu, first_glu - 1))
    glu_col = lambda j: jnp.clip(j - first_glu, 0, n_glu - 1)
    return pl.pallas_call(
        functools.partial(_in_proj_kernel, tiles=(n_plain, n_swish, n_glu)),
        grid_spec=pltpu.PrefetchScalarGridSpec(
            num_scalar_prefetch=2,
            grid=(t // tm, n_tiles),
            in_specs=[
                pl.BlockSpec((tm, D_MODEL), lambda i, j, lo_ref, hi_ref: (i, 0)),
                pl.BlockSpec((1, D_MODEL), lambda i, j, lo_ref, hi_ref: (0, 0)),
                pl.BlockSpec((LANES, D_MODEL), lambda i, j, lo_ref, hi_ref: (0, 0)),
                pl.BlockSpec((pl.Element(half), pl.Element(D_MODEL)),
                             lambda i, j, lo_ref, hi_ref: (lo_ref[j] * ROW_UNIT, 0)),
                pl.BlockSpec((pl.Element(half), pl.Element(D_MODEL)),
                             lambda i, j, lo_ref, hi_ref: (hi_ref[j] * ROW_UNIT, 0)),
            ],
            out_specs=[
                pl.BlockSpec((tm, tn), lambda i, j, lo_ref, hi_ref: (i, proj_col(j))),
                pl.BlockSpec((tm, half), lambda i, j, lo_ref, hi_ref: (i, glu_col(j))),
                pl.BlockSpec((tm, LANES), lambda i, j, lo_ref, hi_ref: (i, 0)),
            ],
            scratch_shapes=[pltpu.VMEM((tm, D_MODEL), BF16)],
        ),
        out_shape=[
            jax.ShapeDtypeStruct((t, N_PROJ), BF16),
            jax.ShapeDtypeStruct((t, D_MODEL), BF16),
            jax.ShapeDtypeStruct((t, LANES), BF16),
        ],
        compiler_params=pltpu.CompilerParams(
            dimension_semantics=("arbitrary", "arbitrary"),
            vmem_limit_bytes=VMEM_LIMIT),
        name="in_proj",
    )(jnp.asarray(np.array(lo, np.int32) // ROW_UNIT), jnp.asarray(np.array(hi, np.int32) // ROW_UNIT),
      h, g, w_z, wt, wt)


def _cum_matrix(n, reverse):
    ri = lax.broadcasted_iota(jnp.int32, (n, n), 0)
    ci = lax.broadcasted_iota(jnp.int32, (n, n), 1)
    tri = jnp.where((ci >= ri) if reverse else (ci <= ri), 1.0, 0.0).astype(BF16)
    return jnp.concatenate([tri, tri], axis=1)


def _log_decay(z, wa, ba):
    lin = jnp.dot(z, wa, preferred_element_type=F32) + ba
    return (jnp.minimum(lin, 0.0) - jnp.log1p(jnp.exp(-jnp.abs(lin)))) / TAU


def _cum_decay(la, cum):
    la_hi = la.astype(BF16)
    la_lo = (la - la_hi.astype(F32)).astype(BF16)
    return jnp.dot(cum, jnp.concatenate([la_hi, la_lo], axis=0), preferred_element_type=F32)


def _gla_front_kernel(k_ref, v_ref, z_ref, wa_ref, ba_ref, s_ref):
    la = _log_decay(z_ref[...], wa_ref[...], ba_ref[...])
    row_id = lax.broadcasted_iota(jnp.int32, la.shape, 0)
    la = jnp.where(row_id >= FRONT, la, 0.0)
    b = _cum_decay(la, _cum_matrix(CHUNK, False))
    k_e = (k_ref[...].astype(F32) * jnp.exp(b[CHUNK - 1:CHUNK, :] - b)).astype(BF16)
    for h in range(HEADS):
        s_ref[h] = lax.dot_general(k_e[:, h * DK:(h + 1) * DK], v_ref[:, h * DV:(h + 1) * DV], _TN,
                                   preferred_element_type=F32)


def _gla_front(proj_front, z_front, wa, ba):
    return pl.pallas_call(
        _gla_front_kernel,
        grid=(1,),
        in_specs=[
            pl.BlockSpec((CHUNK, DQK), lambda i: (0, 1)),
            pl.BlockSpec((CHUNK, DV_TOT), lambda i: (0, 1)),
            pl.BlockSpec((CHUNK, LANES), lambda i: (0, 0)),
            pl.BlockSpec((LANES, DQK), lambda i: (0, 0)),
            pl.BlockSpec((1, DQK), lambda i: (0, 0)),
        ],
        out_specs=pl.BlockSpec((HEADS, DK, DV), lambda i: (0, 0, 0)),
        out_shape=jax.ShapeDtypeStruct((HEADS, DK, DV), F32),
        name="gla_front",
    )(proj_front, proj_front, z_front, wa, ba)


def _gla_kernel(blk_ref, first_ref, q_ref, k_ref, v_ref, z_ref, wa_ref, ba_ref, *rest, reverse):
    if reverse:
        of_ref, r_ref, gg_ref, o_ref, s_ref = rest
    else:
        s0_ref, o_ref, s_ref = rest
    t = pl.program_id(0)
    n = GLA_CHUNK
    chunks = range(GLA_BLOCK_CHUNKS)
    heads = range(HEADS)
    ks = [slice(h * DK, (h + 1) * DK) for h in heads]
    vs = [slice(h * DV, (h + 1) * DV) for h in heads]

    @pl.when(first_ref[t] == 1)
    def _():
        s_ref[...] = jnp.zeros_like(s_ref) if reverse else s0_ref[...]

    ri = lax.broadcasted_iota(jnp.int32, (n, n), 0)
    ci = lax.broadcasted_iota(jnp.int32, (n, n), 1)
    score_mask = (ci > ri) if reverse else (ci <= ri)
    cum = _cum_matrix(n, reverse)
    mid_row = n // 2 if reverse else n // 2 - 1
    tot_row = 0 if reverse else n - 1

    la = _log_decay(z_ref[...], wa_ref[...], ba_ref[...])
    b = [_cum_decay(la[c * n:(c + 1) * n, :], cum) for c in chunks]

    q_in, k_in, q_st, k_st, dec = [], [], [], [], []
    for c in chunks:
        sl = pl.ds(c * n, n)
        b_mid = b[c][mid_row:mid_row + 1, :]
        b_tot = b[c][tot_row:tot_row + 1, :]
        q = q_ref[sl, :].astype(F32) * (DK ** -0.5)
        k = k_ref[sl, :].astype(F32)
        q_in.append((q * jnp.exp(b[c] - b_mid)).astype(BF16))
        k_in.append((k * jnp.exp(b_mid - b[c])).astype(BF16))
        q_st.append((q * jnp.exp(b[c])).astype(BF16))
        k_st.append((k * jnp.exp(b_tot - b[c])).astype(BF16))
        dec.append(jnp.broadcast_to(jnp.exp(b_tot), (LANES, DQK)))

    v = [[v_ref[pl.ds(c * n, n), vs[h]] for h in heads] for c in chunks]
    scores = [[lax.dot_general(q_in[c][:, ks[h]], k_in[c][:, ks[h]], _NT, preferred_element_type=F32)
               for h in heads] for c in chunks]
    probs = [[jnp.where(score_mask, scores[c][h], 0.0).astype(BF16) for h in heads] for c in chunks]
    o_intra = [[jnp.dot(probs[c][h], v[c][h], preferred_element_type=F32) for h in heads] for c in chunks]
    upd = [[lax.dot_general(k_st[c][:, ks[h]], v[c][h], _TN, preferred_element_type=F32) for h in heads]
           for c in chunks]
    dec_t = [[dec[c][:, ks[h]].T for h in heads] for c in chunks]

    order = range(GLA_BLOCK_CHUNKS - 1, -1, -1) if reverse else chunks
    for c in order:
        sl = pl.ds(c * n, n)
        state = [s_ref[h] for h in heads]
        o_inter = [jnp.dot(q_st[c][:, ks[h]], state[h].astype(BF16), preferred_element_type=F32) for h in heads]
        for h in heads:
            s_ref[h] = state[h] * jnp.concatenate([dec_t[c][h]] * (DV // LANES), axis=1) + upd[c][h]
        for h in heads:
            o = o_intra[c][h] + o_inter[h]
            if reverse:
                o = _rms(o + of_ref[sl, vs[h]], gg_ref[:, vs[h]])
                o_ref[sl, vs[h]] = (o * r_ref[sl, vs[h]].astype(F32)).astype(o_ref.dtype)
            else:
                o_ref[sl, vs[h]] = o


def _gla(proj, z, wa, ba, tables, *, reverse, s_front=None, o_fwd=None, g_gla=None):
    t = proj.shape[0]
    blk, first = tables
    rows = lambda width, col: pl.BlockSpec((GLA_ROWS, width), lambda s, blk_ref, first_ref: (blk_ref[s], col))
    const = lambda shape: pl.BlockSpec(shape, lambda s, blk_ref, first_ref: (0,) * len(shape))
    in_specs = [rows(DQK, 0), rows(DQK, 1), rows(DV_TOT, 1), rows(LANES, 0), const((LANES, DQK)), const((1, DQK))]
    args = [proj, proj, proj, z, wa, ba]
    if reverse:
        in_specs += [rows(DV_TOT, 0), rows(DV_TOT, COL_R), const((1, DV_TOT))]
        args += [o_fwd, proj, g_gla]
    else:
        in_specs += [const((HEADS, DK, DV))]
        args += [s_front]
    return pl.pallas_call(
        functools.partial(_gla_kernel, reverse=reverse),
        grid_spec=pltpu.PrefetchScalarGridSpec(
            num_scalar_prefetch=2,
            grid=(blk.shape[0],),
            in_specs=in_specs,
            out_specs=rows(DV_TOT, 0),
            scratch_shapes=[pltpu.VMEM((HEADS, DK, DV), F32)],
        ),
        out_shape=jax.ShapeDtypeStruct((t, DV_TOT), BF16 if reverse else F32),
        compiler_params=pltpu.CompilerParams(
            dimension_semantics=("arbitrary",),
            vmem_limit_bytes=VMEM_LIMIT),
        name="gla_bwd" if reverse else "gla_fwd",
    )(blk, first, *args)


def _gla_tables(n_seq, seq_len, reverse):
    nb = seq_len // GLA_ROWS
    order = np.arange(nb - 1, -1, -1) if reverse else np.arange(nb)
    blk = np.concatenate([s * nb + order for s in range(n_seq)]).astype(np.int32)
    first = np.tile(np.arange(nb) == 0, n_seq).astype(np.int32)
    return jnp.asarray(blk), jnp.asarray(first)


def _mix_kernel(og_ref, ga_ref, gb_ref, h_ref, g_ref, gp_ref, gn_ref, gf_ref,
                wgo_ref, wco_ref, wout_ref, wdw_ref, bdw_ref, lng_ref, lnb_ref, bco_ref, gpost_ref, gpre_ref,
                h1_ref, u2_ref, g_scr, y_scr, yc_scr, m_scr, *, tm, rb, tiles_per_seq, n_tiles):
    i = pl.program_id(0)
    pos = jnp.minimum(i, n_tiles - 1) % tiles_per_seq
    seq_start = pos == 0
    seq_end = pos == tiles_per_seq - 1

    @pl.when(i == 0)
    def _():
        yc_scr[...] = jnp.zeros_like(yc_scr)

    g_prev = jnp.where(seq_start, gf_ref[...], gp_ref[...]).astype(F32)
    g_next = jnp.where(seq_end, 0.0, gn_ref[...].astype(F32))
    g_cur = g_ref[...].astype(F32)
    for c in range(N_CBLK):
        cs = slice(c * LANES, (c + 1) * LANES)
        g_scr[c, 0:HALO, :] = g_prev[:, cs]
        g_scr[c, HALO:HALO + tm, :] = g_cur[:, cs]
        g_scr[c, HALO + tm:, :] = g_next[:, cs]

    def slab(s, carry):
        for cc in range(CBLK_PER_SLAB):
            c = s * CBLK_PER_SLAB + cc
            for r0 in range(0, tm, rb):
                acc = jnp.zeros((rb, LANES), F32)
                for w in range(CONV_W):
                    start = HALO - CONV_PAD + w + r0
                    acc = acc + g_scr[c, start:start + rb, :] * wdw_ref[c, w:w + 1, :]
                y_scr[c, r0:r0 + rb, :] = acc + bdw_ref[c]
        cols = pl.ds(pl.multiple_of(s * MXU_DIM, MXU_DIM), MXU_DIM)
        y_a = jnp.dot(og_ref[...], wgo_ref[s], preferred_element_type=F32)
        y_b = jnp.dot(yc_scr[...], wco_ref[s], preferred_element_type=F32) + bco_ref[s]
        merged = ga_ref[:, cols].astype(F32) * y_a + gb_ref[:, cols].astype(F32) * y_b
        m_scr[s] = merged.astype(BF16)
        return carry

    lax.fori_loop(0, N_SLAB, slab, 0)

    merged = jnp.concatenate([m_scr[s] for s in range(N_SLAB)], axis=1)
    mix = jnp.dot(merged, wout_ref[...], preferred_element_type=F32)
    h1 = h_ref[...] + _rms(mix, gpost_ref[...])
    h1_ref[...] = h1
    u2_ref[...] = _rms(h1, gpre_ref[...]).astype(BF16)

    s1 = jnp.zeros((tm, LANES), F32)
    for c in range(N_CBLK):
        s1 = s1 + y_scr[c]
    mu = jnp.sum(s1, axis=-1, keepdims=True) / D_MODEL
    s2 = jnp.zeros((tm, LANES), F32)
    for c in range(N_CBLK):
        d = y_scr[c] - mu
        s2 = s2 + d * d
    inv = lax.rsqrt(jnp.sum(s2, axis=-1, keepdims=True) / D_MODEL + EPS)
    for c in range(N_CBLK):
        yn = (y_scr[c] - mu) * inv * lng_ref[c] + lnb_ref[c]
        yc_scr[:, c * LANES:(c + 1) * LANES] = (yn * _sigmoid(yn)).astype(BF16)


def _mix(og, proj, glu, glu_front, h, seq_len, wgo, wco, wout, wdw, bdw, lng, lnb, bco, gpost, gpre, tm, rb):
    t = h.shape[0]
    n_tiles = t // tm
    hb = tm // HALO
    last_hblk = t // HALO - 1
    conv_tile = lambda i: jnp.minimum(i, n_tiles - 1)
    proj_tile = lambda i: jnp.maximum(i - 1, 0)
    ptile = lambda col: pl.BlockSpec((tm, D_MODEL), lambda i: (proj_tile(i), col))
    ctile = pl.BlockSpec((tm, D_MODEL), lambda i: (conv_tile(i), 0))
    prev = pl.BlockSpec((HALO, D_MODEL), lambda i: (jnp.maximum(conv_tile(i) * hb - 1, 0), 0))
    nxt = pl.BlockSpec((HALO, D_MODEL), lambda i: (jnp.minimum((conv_tile(i) + 1) * hb, last_hblk), 0))
    meta = pl.BlockSpec((HALO, D_MODEL), lambda i: (FRONT // HALO, 0))
    resident = lambda shape: pl.BlockSpec(shape, lambda i: (0,) * len(shape), pipeline_mode=pl.Buffered(1))
    vec = resident((1, D_MODEL))
    cvec = resident((N_CBLK, 1, LANES))
    return pl.pallas_call(
        functools.partial(_mix_kernel, tm=tm, rb=rb, tiles_per_seq=seq_len // tm, n_tiles=n_tiles),
        grid=(n_tiles + 1,),
        in_specs=[
            ptile(0), ptile(COL_GATE), ptile(COL_GATE + 1), ptile(0),
            ctile, prev, nxt, meta,
            resident((N_SLAB, DV_TOT, MXU_DIM)), resident((N_SLAB, D_MODEL, MXU_DIM)), resident((D_MODEL, D_MODEL)),
            resident((N_CBLK, CONV_W + 1, LANES)), cvec, cvec, cvec, resident((N_SLAB, 1, MXU_DIM)), vec, vec,
        ],
        out_specs=[ptile(0), ptile(0)],
        out_shape=[jax.ShapeDtypeStruct((t, D_MODEL), F32), jax.ShapeDtypeStruct((t, D_MODEL), BF16)],
        scratch_shapes=[
            pltpu.VMEM((N_CBLK, tm + 2 * HALO, LANES), F32),
            pltpu.VMEM((N_CBLK, tm, LANES), F32),
            pltpu.VMEM((tm, D_MODEL), BF16),
            pltpu.VMEM((N_SLAB, tm, MXU_DIM), BF16),
        ],
        compiler_params=pltpu.CompilerParams(
            dimension_semantics=("arbitrary",),
            vmem_limit_bytes=VMEM_LIMIT),
        name="mix",
    )(og, proj, proj, h, glu, glu, glu, glu_front,
      wgo, wco, wout, wdw, bdw, lng, lnb, bco, gpost, gpre)


def _mlp_kernel(u_ref, h_ref, wu_ref, wd_ref, g_ref, o_ref):
    f = pl.program_id(1)

    @pl.when(f == 0)
    def _():
        o_ref[...] = jnp.zeros_like(o_ref)

    up = jnp.dot(u_ref[...], wu_ref[...], preferred_element_type=F32)
    act = jnp.square(jnp.maximum(up, 0.0)).astype(BF16)
    o_ref[...] += jnp.dot(act, wd_ref[...], preferred_element_type=F32)

    @pl.when(f == pl.num_programs(1) - 1)
    def _():
        o_ref[...] = h_ref[...] + _rms(o_ref[...], g_ref[...])


def _mlp(u2, h1, w_up, w_down, g, tm, tf):
    t = u2.shape[0]
    return pl.pallas_call(
        _mlp_kernel,
        grid=(t // tm, D_FF // tf),
        in_specs=[
            pl.BlockSpec((tm, D_MODEL), lambda i, f: (i, 0)),
            pl.BlockSpec((tm, D_MODEL), lambda i, f: (i, 0)),
            pl.BlockSpec((D_MODEL, tf), lambda i, f: (0, f)),
            pl.BlockSpec((tf, D_MODEL), lambda i, f: (f, 0)),
            pl.BlockSpec((1, D_MODEL), lambda i, f: (0, 0)),
        ],
        out_specs=pl.BlockSpec((tm, D_MODEL), lambda i, f: (i, 0)),
        out_shape=jax.ShapeDtypeStruct((t, D_MODEL), F32),
        compiler_params=pltpu.CompilerParams(
            dimension_semantics=("arbitrary", "arbitrary"),
            vmem_limit_bytes=VMEM_LIMIT),
        name="mlp",
    )(u2, h1, w_up, w_down, g)


def _cblocks(vec):
    return vec.astype(F32).reshape(N_CBLK, 1, LANES)


def _col_slabs(w):
    return w.reshape(w.shape[0], N_SLAB, MXU_DIM).transpose(1, 0, 2).astype(BF16)


def kernel(x_prompt, x_sample, meta_tokens, g_pre_mix, w_in, w_a2_f, b_a_f, w_a2_b, b_a_b, g_gla,
           w_gla_o, w_dw, b_dw, ln_g, ln_b, w_conv_o, b_conv_o, w_out, g_post_mix, g_pre_mlp,
           w_up, w_down, g_post_mlp):
    assert w_in.shape[0] == 1
    l = 0
    tm_proj, tn_proj, tm_mix, rb_mix, tm_mlp, tf_mlp = 1024, 2048, 256, 64, 512, 1024
    row = lambda v: v.astype(F32).reshape(1, -1)
    z0 = N_HEAD_COLS

    w_main = w_in[l].T.astype(BF16)
    w_z = jnp.pad(w_in[l].T[z0:z0 + 2 * GATE_RANK], ((0, LANES - 2 * GATE_RANK), (0, 0))).astype(BF16)
    wa_f = jnp.pad(w_a2_f[l], ((0, LANES - GATE_RANK), (0, 0))).astype(BF16)
    wa_b = jnp.pad(w_a2_b[l], ((GATE_RANK, LANES - 2 * GATE_RANK), (0, 0))).astype(BF16)
    wdw = jnp.pad(w_dw[l].astype(F32), ((0, 1), (0, 0))).reshape(CONV_W + 1, N_CBLK, LANES).transpose(1, 0, 2)
    w_go, w_co, w_o = _col_slabs(w_gla_o[l]), _col_slabs(w_conv_o[l]), w_out[l].astype(BF16)
    b_co = b_conv_o[l].astype(F32).reshape(N_SLAB, 1, MXU_DIM)
    w_u, w_d = w_up[l].astype(BF16), w_down[l].astype(BF16)

    front = jnp.concatenate([jnp.zeros((FRONT, D_MODEL), F32), meta_tokens.astype(F32)], axis=0)
    proj_front, glu_front, z_front = _in_proj(front, row(g_pre_mix[l]), w_main, w_z, CHUNK, tn_proj)
    s_front = _gla_front(proj_front, z_front, wa_f, row(b_a_f[l]))

    outs = []
    for x in (x_prompt, x_sample):
        bsz, seq_len, _ = x.shape
        assert seq_len % GLA_ROWS == 0 and seq_len % tm_mix == 0
        h = x.reshape(bsz * seq_len, D_MODEL)
        proj, glu, z = _in_proj(h, row(g_pre_mix[l]), w_main, w_z, tm_proj, tn_proj)
        o_f = _gla(proj, z, wa_f, row(b_a_f[l]), _gla_tables(bsz, seq_len, False), reverse=False, s_front=s_front)
        og = _gla(proj, z, wa_b, row(b_a_b[l]), _gla_tables(bsz, seq_len, True), reverse=True,
                  o_fwd=o_f, g_gla=row(g_gla[l]))
        h1, u2 = _mix(og, proj, glu, glu_front, h, seq_len, w_go, w_co, w_o, wdw, _cblocks(b_dw[l]),
                      _cblocks(ln_g[l]), _cblocks(ln_b[l]), b_co, row(g_post_mix[l]),
                      row(g_pre_mlp[l]), tm_mix, rb_mix)
        y = _mlp(u2, h1, w_u, w_d, row(g_post_mlp[l]), tm_mlp, tf_mlp)
        outs.append(y.reshape(bsz, seq_len, D_MODEL))
    return tuple(outs)
```

```python
import functools

import jax
import jax.numpy as jnp
import numpy as np
from jax import lax
from jax.experimental import pallas as pl
from jax.experimental.pallas import tpu as pltpu

F32 = jnp.float32
BF16 = jnp.bfloat16

D_MODEL = 2048
N_META = 16
HEADS = 4
DK = 256
DV = 512
DQK = HEADS * DK
DV_TOT = HEADS * DV
GATE_RANK = 16
TAU = 16.0
CHUNK = 64
FRONT = CHUNK - N_META
CONV_W = 31
CONV_PAD = CONV_W // 2
D_FF = 4 * D_MODEL
EPS = 1e-6

LANES = 128
MXU_DIM = 256
HALO = 16
N_CBLK = D_MODEL // LANES
N_SLAB = D_MODEL // MXU_DIM
CBLK_PER_SLAB = MXU_DIM // LANES
N_HEAD_COLS = 2 * DQK + 2 * DV_TOT
N_MAIN = N_HEAD_COLS + 4 * D_MODEL
N_PROJ = N_HEAD_COLS + 2 * D_MODEL
COL_R = (2 * DQK + DV_TOT) // D_MODEL
COL_GATE = N_HEAD_COLS // D_MODEL
ROW_UNIT = 2 * GATE_RANK

GLA_CHUNK = 2 * CHUNK
GLA_BLOCK_CHUNKS = 4
GLA_ROWS = GLA_BLOCK_CHUNKS * GLA_CHUNK

VMEM_LIMIT = 58 * 1024 * 1024

_NT = (((1,), (1,)), ((), ()))
_TN = (((0,), (0,)), ((), ()))


def _sigmoid(x):
    return 1.0 / (1.0 + jnp.exp(-x))


def _rms(x, g):
    ms = jnp.mean(x * x, axis=-1, keepdims=True)
    return x * lax.rsqrt(ms + EPS) * g


def _in_proj_kernel(lo_ref, hi_ref, x_ref, g_ref, wz_ref, wlo_ref, whi_ref, o_ref, glu_ref, z_ref, u_ref, *, tiles):
    n_plain, n_swish, n_glu = tiles
    j = pl.program_id(1)
    half = wlo_ref.shape[0]

    @pl.when(j == 0)
    def _():
        u = _rms(x_ref[...], g_ref[...]).astype(BF16)
        u_ref[...] = u
        z_ref[...] = lax.dot_general(u, wz_ref[...], _NT, preferred_element_type=F32).astype(BF16)

    is_glu = jnp.logical_and(j >= n_plain + n_swish, j < n_plain + n_swish + n_glu)

    @pl.when(jnp.logical_not(is_glu))
    def _():
        for w_ref, cols in ((wlo_ref, slice(0, half)), (whi_ref, slice(half, 2 * half))):
            acc = lax.dot_general(u_ref[...], w_ref[...], _NT, preferred_element_type=F32)
            sig = _sigmoid(acc)
            out = jnp.where(j < n_plain, acc, jnp.where(j < n_plain + n_swish, acc * sig, sig))
            o_ref[:, cols] = out.astype(BF16)

    @pl.when(is_glu)
    def _():
        val = lax.dot_general(u_ref[...], wlo_ref[...], _NT, preferred_element_type=F32)
        gate = lax.dot_general(u_ref[...], whi_ref[...], _NT, preferred_element_type=F32)
        glu_ref[...] = (val * _sigmoid(gate)).astype(BF16)


def _in_proj(h, g, wt, w_z, tm, tn):
    t = h.shape[0]
    half = tn // 2
    assert (2 * DQK + DV_TOT) % tn == 0 and DV_TOT % tn == 0
    n_plain, n_swish, n_glu = (2 * DQK + DV_TOT) // tn, DV_TOT // tn, 2 * D_MODEL // tn
    first_glu = n_plain + n_swish
    z1 = N_HEAD_COLS + 2 * GATE_RANK
    lo = [j * tn for j in range(first_glu)] + [z1 + c * half for c in range(n_glu)]
    hi = [j * tn + half for j in range(first_glu)] + [z1 + D_MODEL + c * half for c in range(n_glu)]
    for s in range(z1 + 2 * D_MODEL, wt.shape[0], tn):
        lo.append(s)
        hi.append(s + half)
    n_tiles = len(lo)
    assert n_tiles * tn == N_MAIN and all(s % ROW_UNIT == 0 for s in lo + hi)
    proj_col = lambda j: jnp.where(j < first_glu, j, jnp.maximum(j - n_glu, first_glu - 1))
    glu_col = lambda j: jnp.clip(j - first_glu, 0, n_glu - 1)
    return pl.pallas_call(
        functools.partial(_in_proj_kernel, tiles=(n_plain, n_swish, n_glu)),
        grid_spec=pltpu.PrefetchScalarGridSpec(
            num_scalar_prefetch=2,
            grid=(t // tm, n_tiles),
            in_specs=[
                pl.BlockSpec((tm, D_MODEL), lambda i, j, lo_ref, hi_ref: (i, 0)),
                pl.BlockSpec((1, D_MODEL), lambda i, j, lo_ref, hi_ref: (0, 0)),
                pl.BlockSpec((LANES, D_MODEL), lambda i, j, lo_ref, hi_ref: (0, 0)),
                pl.BlockSpec((pl.Element(half), pl.Element(D_MODEL)),
                             lambda i, j, lo_ref, hi_ref: (lo_ref[j] * ROW_UNIT, 0)),
                pl.BlockSpec((pl.Element(half), pl.Element(D_MODEL)),
                             lambda i, j, lo_ref, hi_ref: (hi_ref[j] * ROW_UNIT, 0)),
            ],
            out_specs=[
                pl.BlockSpec((tm, tn), lambda i, j, lo_ref, hi_ref: (i, proj_col(j))),
                pl.BlockSpec((tm, half), lambda i, j, lo_ref, hi_ref: (i, glu_col(j))),
                pl.BlockSpec((tm, LANES), lambda i, j, lo_ref, hi_ref: (i, 0)),
            ],
            scratch_shapes=[pltpu.VMEM((tm, D_MODEL), BF16)],
        ),
        out_shape=[
            jax.ShapeDtypeStruct((t, N_PROJ), BF16),
            jax.ShapeDtypeStruct((t, D_MODEL), BF16),
            jax.ShapeDtypeStruct((t, LANES), BF16),
        ],
        compiler_params=pltpu.CompilerParams(
            dimension_semantics=("arbitrary", "arbitrary"),
            vmem_limit_bytes=VMEM_LIMIT),
        name="in_proj",
    )(jnp.asarray(np.array(lo, np.int32) // ROW_UNIT), jnp.asarray(np.array(hi, np.int32) // ROW_UNIT),
      h, g, w_z, wt, wt)


def _cum_matrix(n, reverse):
    ri = lax.broadcasted_iota(jnp.int32, (n, n), 0)
    ci = lax.broadcasted_iota(jnp.int32, (n, n), 1)
    tri = jnp.where((ci >= ri) if reverse else (ci <= ri), 1.0, 0.0).astype(BF16)
    return jnp.concatenate([tri, tri], axis=1)


def _log_decay(z, wa, ba):
    lin = jnp.dot(z, wa, preferred_element_type=F32) + ba
    return (jnp.minimum(lin, 0.0) - jnp.log(1.0 + jnp.exp(-jnp.abs(lin)))) * (1.0 / TAU)


def _cum_decay(la, cum):
    la_hi = la.astype(BF16)
    la_lo = (la - la_hi.astype(F32)).astype(BF16)
    return jnp.dot(cum, jnp.concatenate([la_hi, la_lo], axis=0), preferred_element_type=F32)


def _gla_front_kernel(k_ref, v_ref, z_ref, wa_ref, ba_ref, s_ref):
    la = _log_decay(z_ref[...], wa_ref[...], ba_ref[...])
    row_id = lax.broadcasted_iota(jnp.int32, la.shape, 0)
    la = jnp.where(row_id >= FRONT, la, 0.0)
    b = _cum_decay(la, _cum_matrix(CHUNK, False))
    k_e = (k_ref[...].astype(F32) * jnp.exp(b[CHUNK - 1:CHUNK, :] - b)).astype(BF16)
    for h in range(HEADS):
        s_ref[h] = lax.dot_general(k_e[:, h * DK:(h + 1) * DK], v_ref[:, h * DV:(h + 1) * DV], _TN,
                                   preferred_element_type=F32)


def _gla_front(proj_front, z_front, wa, ba):
    return pl.pallas_call(
        _gla_front_kernel,
        grid=(1,),
        in_specs=[
            pl.BlockSpec((CHUNK, DQK), lambda i: (0, 1)),
            pl.BlockSpec((CHUNK, DV_TOT), lambda i: (0, 1)),
            pl.BlockSpec((CHUNK, LANES), lambda i: (0, 0)),
            pl.BlockSpec((LANES, DQK), lambda i: (0, 0)),
            pl.BlockSpec((1, DQK), lambda i: (0, 0)),
        ],
        out_specs=pl.BlockSpec((HEADS, DK, DV), lambda i: (0, 0, 0)),
        out_shape=jax.ShapeDtypeStruct((HEADS, DK, DV), F32),
        name="gla_front",
    )(proj_front, proj_front, z_front, wa, ba)


def _gla_kernel(blk_ref, first_ref, q_ref, k_ref, v_ref, z_ref, wa_ref, ba_ref, *rest, reverse):
    if reverse:
        of_ref, r_ref, gg_ref, o_ref, s_ref = rest
    else:
        s0_ref, o_ref, s_ref = rest
    t = pl.program_id(0)
    n = GLA_CHUNK
    chunks = range(GLA_BLOCK_CHUNKS)
    heads = range(HEADS)
    ks = [slice(h * DK, (h + 1) * DK) for h in heads]
    vs = [slice(h * DV, (h + 1) * DV) for h in heads]

    @pl.when(first_ref[t] == 1)
    def _():
        s_ref[...] = jnp.zeros_like(s_ref) if reverse else s0_ref[...]

    ri = lax.broadcasted_iota(jnp.int32, (n, n), 0)
    ci = lax.broadcasted_iota(jnp.int32, (n, n), 1)
    score_mask = (ci > ri) if reverse else (ci <= ri)
    cum = _cum_matrix(n, reverse)
    mid_row = n // 2 if reverse else n // 2 - 1
    tot_row = 0 if reverse else n - 1

    la = _log_decay(z_ref[...], wa_ref[...], ba_ref[...])
    b = [_cum_decay(la[c * n:(c + 1) * n, :], cum) for c in chunks]

    q_in, k_in, q_st, k_st, dec = [], [], [], [], []
    for c in chunks:
        sl = pl.ds(c * n, n)
        b_mid = b[c][mid_row:mid_row + 1, :]
        b_tot = b[c][tot_row:tot_row + 1, :]
        q = q_ref[sl, :].astype(F32) * (DK ** -0.5)
        k = k_ref[sl, :].astype(F32)
        q_in.append((q * jnp.exp(b[c] - b_mid)).astype(BF16))
        k_in.append((k * jnp.exp(b_mid - b[c])).astype(BF16))
        q_st.append((q * jnp.exp(b[c])).astype(BF16))
        k_st.append((k * jnp.exp(b_tot - b[c])).astype(BF16))
        dec.append(jnp.broadcast_to(jnp.exp(b_tot), (LANES, DQK)))

    v = [[v_ref[pl.ds(c * n, n), vs[h]] for h in heads] for c in chunks]
    scores = [[lax.dot_general(q_in[c][:, ks[h]], k_in[c][:, ks[h]], _NT, preferred_element_type=F32)
               for h in heads] for c in chunks]
    probs = [[jnp.where(score_mask, scores[c][h], 0.0).astype(BF16) for h in heads] for c in chunks]
    dec_t = [[dec[c][:, ks[h]].T for h in heads] for c in chunks]

    order = range(GLA_BLOCK_CHUNKS - 1, -1, -1) if reverse else chunks
    for c in order:
        sl = pl.ds(c * n, n)
        state = [s_ref[h] for h in heads]
        o_all = [jnp.dot(probs[c][h], v[c][h], preferred_element_type=F32)
                 + jnp.dot(q_st[c][:, ks[h]], state[h].astype(BF16), preferred_element_type=F32) for h in heads]
        for h in heads:
            upd = lax.dot_general(k_st[c][:, ks[h]], v[c][h], _TN, preferred_element_type=F32)
            s_ref[h] = state[h] * jnp.concatenate([dec_t[c][h]] * (DV // LANES), axis=1) + upd
        for h in heads:
            o = o_all[h]
            if reverse:
                o = _rms(o + of_ref[sl, vs[h]], gg_ref[:, vs[h]])
                o_ref[sl, vs[h]] = (o * r_ref[sl, vs[h]].astype(F32)).astype(o_ref.dtype)
            else:
                o_ref[sl, vs[h]] = o


def _gla(proj, z, wa, ba, tables, *, reverse, s_front=None, o_fwd=None, g_gla=None):
    t = proj.shape[0]
    blk, first = tables
    rows = lambda width, col: pl.BlockSpec((GLA_ROWS, width), lambda s, blk_ref, first_ref: (blk_ref[s], col))
    const = lambda shape: pl.BlockSpec(shape, lambda s, blk_ref, first_ref: (0,) * len(shape))
    in_specs = [rows(DQK, 0), rows(DQK, 1), rows(DV_TOT, 1), rows(LANES, 0), const((LANES, DQK)), const((1, DQK))]
    args = [proj, proj, proj, z, wa, ba]
    if reverse:
        in_specs += [rows(DV_TOT, 0), rows(DV_TOT, COL_R), const((1, DV_TOT))]
        args += [o_fwd, proj, g_gla]
    else:
        in_specs += [const((HEADS, DK, DV))]
        args += [s_front]
    return pl.pallas_call(
        functools.partial(_gla_kernel, reverse=reverse),
        grid_spec=pltpu.PrefetchScalarGridSpec(
            num_scalar_prefetch=2,
            grid=(blk.shape[0],),
            in_specs=in_specs,
            out_specs=rows(DV_TOT, 0),
            scratch_shapes=[pltpu.VMEM((HEADS, DK, DV), F32)],
        ),
        out_shape=jax.ShapeDtypeStruct((t, DV_TOT), BF16 if reverse else F32),
        compiler_params=pltpu.CompilerParams(
            dimension_semantics=("arbitrary",),
            vmem_limit_bytes=VMEM_LIMIT),
        name="gla_bwd" if reverse else "gla_fwd",
    )(blk, first, *args)


def _gla_tables(n_seq, seq_len, reverse):
    nb = seq_len // GLA_ROWS
    order = np.arange(nb - 1, -1, -1) if reverse else np.arange(nb)
    blk = np.concatenate([s * nb + order for s in range(n_seq)]).astype(np.int32)
    first = np.tile(np.arange(nb) == 0, n_seq).astype(np.int32)
    return jnp.asarray(blk), jnp.asarray(first)


def _mix_kernel(og_ref, ga_ref, gb_ref, h_ref, g_ref, gp_ref, gn_ref, gf_ref,
                wgo_ref, wco_ref, wout_ref, wdw_ref, bdw_ref, lng_ref, lnb_ref, bco_ref, gpost_ref, gpre_ref,
                h1_ref, u2_ref, g_scr, y_scr, yc_scr, m_scr, *, tm, rb, tiles_per_seq, n_tiles):
    i = pl.program_id(0)
    pos = jnp.minimum(i, n_tiles - 1) % tiles_per_seq
    seq_start = pos == 0
    seq_end = pos == tiles_per_seq - 1

    @pl.when(i == 0)
    def _():
        yc_scr[...] = jnp.zeros_like(yc_scr)

    g_prev = jnp.where(seq_start, gf_ref[...], gp_ref[...]).astype(F32)
    g_next = jnp.where(seq_end, 0.0, gn_ref[...].astype(F32))
    g_cur = g_ref[...].astype(F32)
    for c in range(N_CBLK):
        cs = slice(c * LANES, (c + 1) * LANES)
        g_scr[c, 0:HALO, :] = g_prev[:, cs]
        g_scr[c, HALO:HALO + tm, :] = g_cur[:, cs]
        g_scr[c, HALO + tm:, :] = g_next[:, cs]

    def slab(s, carry):
        for cc in range(CBLK_PER_SLAB):
            c = s * CBLK_PER_SLAB + cc
            for r0 in range(0, tm, rb):
                acc = jnp.zeros((rb, LANES), F32)
                for w in range(CONV_W):
                    start = HALO - CONV_PAD + w + r0
                    acc = acc + g_scr[c, start:start + rb, :] * wdw_ref[c, w:w + 1, :]
                y_scr[c, r0:r0 + rb, :] = acc + bdw_ref[c]
        cols = pl.ds(pl.multiple_of(s * MXU_DIM, MXU_DIM), MXU_DIM)
        y_a = jnp.dot(og_ref[...], wgo_ref[s], preferred_element_type=F32)
        y_b = jnp.dot(yc_scr[...], wco_ref[s], preferred_element_type=F32) + bco_ref[s]
        merged = ga_ref[:, cols].astype(F32) * y_a + gb_ref[:, cols].astype(F32) * y_b
        m_scr[s] = merged.astype(BF16)
        return carry

    lax.fori_loop(0, N_SLAB, slab, 0)

    merged = jnp.concatenate([m_scr[s] for s in range(N_SLAB)], axis=1)
    mix = jnp.dot(merged, wout_ref[...], preferred_element_type=F32)
    h1 = h_ref[...] + _rms(mix, gpost_ref[...])
    h1_ref[...] = h1
    u2_ref[...] = _rms(h1, gpre_ref[...]).astype(BF16)

    s1 = jnp.zeros((tm, LANES), F32)
    for c in range(N_CBLK):
        s1 = s1 + y_scr[c]
    mu = jnp.sum(s1, axis=-1, keepdims=True) / D_MODEL
    s2 = jnp.zeros((tm, LANES), F32)
    for c in range(N_CBLK):
        d = y_scr[c] - mu
        s2 = s2 + d * d
    inv = lax.rsqrt(jnp.sum(s2, axis=-1, keepdims=True) / D_MODEL + EPS)
    for c in range(N_CBLK):
        yn = (y_scr[c] - mu) * inv * lng_ref[c] + lnb_ref[c]
        yc_scr[:, c * LANES:(c + 1) * LANES] = (yn * _sigmoid(yn)).astype(BF16)


def _mix(og, proj, glu, glu_front, h, seq_len, wgo, wco, wout, wdw, bdw, lng, lnb, bco, gpost, gpre, tm, rb):
    t = h.shape[0]
    n_tiles = t // tm
    hb = tm // HALO
    last_hblk = t // HALO - 1
    conv_tile = lambda i: jnp.minimum(i, n_tiles - 1)
    proj_tile = lambda i: jnp.maximum(i - 1, 0)
    ptile = lambda col: pl.BlockSpec((tm, D_MODEL), lambda i: (proj_tile(i), col))
    ctile = pl.BlockSpec((tm, D_MODEL), lambda i: (conv_tile(i), 0))
    prev = pl.BlockSpec((HALO, D_MODEL), lambda i: (jnp.maximum(conv_tile(i) * hb - 1, 0), 0))
    nxt = pl.BlockSpec((HALO, D_MODEL), lambda i: (jnp.minimum((conv_tile(i) + 1) * hb, last_hblk), 0))
    meta = pl.BlockSpec((HALO, D_MODEL), lambda i: (FRONT // HALO, 0))
    resident = lambda shape: pl.BlockSpec(shape, lambda i: (0,) * len(shape), pipeline_mode=pl.Buffered(1))
    vec = resident((1, D_MODEL))
    cvec = resident((N_CBLK, 1, LANES))
    return pl.pallas_call(
        functools.partial(_mix_kernel, tm=tm, rb=rb, tiles_per_seq=seq_len // tm, n_tiles=n_tiles),
        grid=(n_tiles + 1,),
        in_specs=[
            ptile(0), ptile(COL_GATE), ptile(COL_GATE + 1), ptile(0),
            ctile, prev, nxt, meta,
            resident((N_SLAB, DV_TOT, MXU_DIM)), resident((N_SLAB, D_MODEL, MXU_DIM)), resident((D_MODEL, D_MODEL)),
            resident((N_CBLK, CONV_W + 1, LANES)), cvec, cvec, cvec, resident((N_SLAB, 1, MXU_DIM)), vec, vec,
        ],
        out_specs=[ptile(0), ptile(0)],
        out_shape=[jax.ShapeDtypeStruct((t, D_MODEL), F32), jax.ShapeDtypeStruct((t, D_MODEL), BF16)],
        scratch_shapes=[
            pltpu.VMEM((N_CBLK, tm + 2 * HALO, LANES), F32),
            pltpu.VMEM((N_CBLK, tm, LANES), F32),
            pltpu.VMEM((tm, D_MODEL), BF16),
            pltpu.VMEM((N_SLAB, tm, MXU_DIM), BF16),
        ],
        compiler_params=pltpu.CompilerParams(
            dimension_semantics=("arbitrary",),
            vmem_limit_bytes=VMEM_LIMIT),
        name="mix",
    )(og, proj, proj, h, glu, glu, glu, glu_front,
      wgo, wco, wout, wdw, bdw, lng, lnb, bco, gpost, gpre)


def _mlp_kernel(u_ref, h_ref, wu_ref, wd_ref, g_ref, o_ref):
    f = pl.program_id(1)

    @pl.when(f == 0)
    def _():
        o_ref[...] = jnp.zeros_like(o_ref)

    up = jnp.dot(u_ref[...], wu_ref[...], preferred_element_type=F32)
    act = jnp.square(jnp.maximum(up, 0.0)).astype(BF16)
    o_ref[...] += jnp.dot(act, wd_ref[...], preferred_element_type=F32)

    @pl.when(f == pl.num_programs(1) - 1)
    def _():
        o_ref[...] = h_ref[...] + _rms(o_ref[...], g_ref[...])


def _mlp(u2, h1, w_up, w_down, g, tm, tf):
    t = u2.shape[0]
    return pl.pallas_call(
        _mlp_kernel,
        grid=(t // tm, D_FF // tf),
        in_specs=[
            pl.BlockSpec((tm, D_MODEL), lambda i, f: (i, 0)),
            pl.BlockSpec((tm, D_MODEL), lambda i, f: (i, 0)),
            pl.BlockSpec((D_MODEL, tf), lambda i, f: (0, f)),
            pl.BlockSpec((tf, D_MODEL), lambda i, f: (f, 0)),
            pl.BlockSpec((1, D_MODEL), lambda i, f: (0, 0)),
        ],
        out_specs=pl.BlockSpec((tm, D_MODEL), lambda i, f: (i, 0)),
        out_shape=jax.ShapeDtypeStruct((t, D_MODEL), F32),
        compiler_params=pltpu.CompilerParams(
            dimension_semantics=("arbitrary", "arbitrary"),
            vmem_limit_bytes=VMEM_LIMIT),
        name="mlp",
    )(u2, h1, w_up, w_down, g)


def _cblocks(vec):
    return vec.astype(F32).reshape(N_CBLK, 1, LANES)


def _col_slabs(w):
    return w.reshape(w.shape[0], N_SLAB, MXU_DIM).transpose(1, 0, 2).astype(BF16)


def kernel(x_prompt, x_sample, meta_tokens, g_pre_mix, w_in, w_a2_f, b_a_f, w_a2_b, b_a_b, g_gla,
           w_gla_o, w_dw, b_dw, ln_g, ln_b, w_conv_o, b_conv_o, w_out, g_post_mix, g_pre_mlp,
           w_up, w_down, g_post_mlp):
    assert w_in.shape[0] == 1
    l = 0
    tm_proj, tn_proj, tm_mix, rb_mix, tm_mlp, tf_mlp = 1024, 2048, 256, 64, 512, 1024
    row = lambda v: v.astype(F32).reshape(1, -1)
    z0 = N_HEAD_COLS

    w_main = w_in[l].T.astype(BF16)
    w_z = jnp.pad(w_in[l].T[z0:z0 + 2 * GATE_RANK], ((0, LANES - 2 * GATE_RANK), (0, 0))).astype(BF16)
    wa_f = jnp.pad(w_a2_f[l], ((0, LANES - GATE_RANK), (0, 0))).astype(BF16)
    wa_b = jnp.pad(w_a2_b[l], ((GATE_RANK, LANES - 2 * GATE_RANK), (0, 0))).astype(BF16)
    wdw = jnp.pad(w_dw[l].astype(F32), ((0, 1), (0, 0))).reshape(CONV_W + 1, N_CBLK, LANES).transpose(1, 0, 2)
    w_go, w_co, w_o = _col_slabs(w_gla_o[l]), _col_slabs(w_conv_o[l]), w_out[l].astype(BF16)
    b_co = b_conv_o[l].astype(F32).reshape(N_SLAB, 1, MXU_DIM)
    w_u, w_d = w_up[l].astype(BF16), w_down[l].astype(BF16)

    front = jnp.concatenate([jnp.zeros((FRONT, D_MODEL), F32), meta_tokens.astype(F32)], axis=0)
    proj_front, glu_front, z_front = _in_proj(front, row(g_pre_mix[l]), w_main, w_z, CHUNK, tn_proj)
    s_front = _gla_front(proj_front, z_front, wa_f, row(b_a_f[l]))

    outs = []
    for x in (x_prompt, x_sample):
        bsz, seq_len, _ = x.shape
        assert seq_len % GLA_ROWS == 0 and seq_len % tm_mix == 0
        h = x.reshape(bsz * seq_len, D_MODEL)
        proj, glu, z = _in_proj(h, row(g_pre_mix[l]), w_main, w_z, tm_proj, tn_proj)
        o_f = _gla(proj, z, wa_f, row(b_a_f[l]), _gla_tables(bsz, seq_len, False), reverse=False, s_front=s_front)
        og = _gla(proj, z, wa_b, row(b_a_b[l]), _gla_tables(bsz, seq_len, True), reverse=True,
                  o_fwd=o_f, g_gla=row(g_gla[l]))
        h1, u2 = _mix(og, proj, glu, glu_front, h, seq_len, w_go, w_co, w_o, wdw, _cblocks(b_dw[l]),
                      _cblocks(ln_g[l]), _cblocks(ln_b[l]), b_co, row(g_post_mix[l]),
                      row(g_pre_mlp[l]), tm_mix, rb_mix)
        y = _mlp(u2, h1, w_u, w_d, row(g_post_mlp[l]), tm_mlp, tf_mlp)
        outs.append(y.reshape(bsz, seq_len, D_MODEL))
    return tuple(outs)
```

```python
import functools

import jax
import jax.numpy as jnp
import numpy as np
from jax import lax
from jax.experimental import pallas as pl
from jax.experimental.pallas import tpu as pltpu

F32 = jnp.float32
BF16 = jnp.bfloat16

D_MODEL = 2048
N_META = 16
HEADS = 4
DK = 256
DV = 512
DQK = HEADS * DK
DV_TOT = HEADS * DV
GATE_RANK = 16
TAU = 16.0
CHUNK = 64
FRONT = CHUNK - N_META
CONV_W = 31
CONV_PAD = CONV_W // 2
D_FF = 4 * D_MODEL
EPS = 1e-6

LANES = 128
MXU_DIM = 256
HALO = 16
N_CBLK = D_MODEL // LANES
N_SLAB = D_MODEL // MXU_DIM
CBLK_PER_SLAB = MXU_DIM // LANES
N_HEAD_COLS = 2 * DQK + 2 * DV_TOT
N_MAIN = N_HEAD_COLS + 4 * D_MODEL
N_PROJ = N_HEAD_COLS + 2 * D_MODEL
COL_R = (2 * DQK + DV_TOT) // D_MODEL
COL_GATE = N_HEAD_COLS // D_MODEL
ROW_UNIT = 2 * GATE_RANK
PIECE_ROWS = 512

GLA_CHUNK = 2 * CHUNK
GLA_BLOCK_CHUNKS = 4
GLA_ROWS = GLA_BLOCK_CHUNKS * GLA_CHUNK

VMEM_LIMIT = 58 * 1024 * 1024

_NT = (((1,), (1,)), ((), ()))
_TN = (((0,), (0,)), ((), ()))


def _sigmoid(x):
    return 1.0 / (1.0 + jnp.exp(-x))


def _rms(x, g):
    ms = jnp.mean(x * x, axis=-1, keepdims=True)
    return x * lax.rsqrt(ms + EPS) * g


def _in_proj_kernel(lo_ref, hi_ref, x_ref, g_ref, wz_ref, wlo_ref, whi_ref, o_ref, glu_ref, z_ref, u_ref, *, tiles):
    n_plain, n_swish, n_glu = tiles
    j = pl.program_id(1)
    half = wlo_ref.shape[0]
    tm = u_ref.shape[0]
    rows_per_piece = min(tm, PIECE_ROWS)

    @pl.when(j == 0)
    def _():
        u = _rms(x_ref[...], g_ref[...]).astype(BF16)
        u_ref[...] = u
        z_ref[...] = lax.dot_general(u, wz_ref[...], _NT, preferred_element_type=F32).astype(BF16)

    is_glu = jnp.logical_and(j >= n_plain + n_swish, j < n_plain + n_swish + n_glu)

    def project(activation):
        for w_ref, base in ((wlo_ref, 0), (whi_ref, half)):
            for lo in range(0, half, MXU_DIM):
                for r0 in range(0, tm, rows_per_piece):
                    rs = slice(r0, r0 + rows_per_piece)
                    acc = lax.dot_general(u_ref[rs, :], w_ref[lo:lo + MXU_DIM, :], _NT, preferred_element_type=F32)
                    o_ref[rs, base + lo:base + lo + MXU_DIM] = activation(acc).astype(BF16)

    pl.when(j < n_plain)(lambda: project(lambda a: a))
    pl.when(jnp.logical_and(j >= n_plain, j < n_plain + n_swish))(lambda: project(lambda a: a * _sigmoid(a)))
    pl.when(j >= n_plain + n_swish + n_glu)(lambda: project(_sigmoid))

    @pl.when(is_glu)
    def _():
        for lo in range(0, half, MXU_DIM):
            cols = slice(lo, lo + MXU_DIM)
            for r0 in range(0, tm, rows_per_piece):
                rs = slice(r0, r0 + rows_per_piece)
                val = lax.dot_general(u_ref[rs, :], wlo_ref[cols, :], _NT, preferred_element_type=F32)
                gate = lax.dot_general(u_ref[rs, :], whi_ref[cols, :], _NT, preferred_element_type=F32)
                glu_ref[rs, cols] = (val * _sigmoid(gate)).astype(BF16)


def _in_proj(h, g, wt, w_z, tm, tn):
    t = h.shape[0]
    half = tn // 2
    assert (2 * DQK + DV_TOT) % tn == 0 and DV_TOT % tn == 0
    n_plain, n_swish, n_glu = (2 * DQK + DV_TOT) // tn, DV_TOT // tn, 2 * D_MODEL // tn
    first_glu = n_plain + n_swish
    z1 = N_HEAD_COLS + 2 * GATE_RANK
    lo = [j * tn for j in range(first_glu)] + [z1 + c * half for c in range(n_glu)]
    hi = [j * tn + half for j in range(first_glu)] + [z1 + D_MODEL + c * half for c in range(n_glu)]
    for s in range(z1 + 2 * D_MODEL, wt.shape[0], tn):
        lo.append(s)
        hi.append(s + half)
    n_tiles = len(lo)
    assert n_tiles * tn == N_MAIN and all(s % ROW_UNIT == 0 for s in lo + hi)
    proj_col = lambda j: jnp.where(j < first_glu, j, jnp.maximum(j - n_glu, first_glu - 1))
    glu_col = lambda j: jnp.clip(j - first_glu, 0, n_glu - 1)
    return pl.pallas_call(
        functools.partial(_in_proj_kernel, tiles=(n_plain, n_swish, n_glu)),
        grid_spec=pltpu.PrefetchScalarGridSpec(
            num_scalar_prefetch=2,
            grid=(t // tm, n_tiles),
            in_specs=[
                pl.BlockSpec((tm, D_MODEL), lambda i, j, lo_ref, hi_ref: (i, 0)),
                pl.BlockSpec((1, D_MODEL), lambda i, j, lo_ref, hi_ref: (0, 0)),
                pl.BlockSpec((LANES, D_MODEL), lambda i, j, lo_ref, hi_ref: (0, 0)),
                pl.BlockSpec((pl.Element(half), pl.Element(D_MODEL)),
                             lambda i, j, lo_ref, hi_ref: (lo_ref[j] * ROW_UNIT, 0)),
                pl.BlockSpec((pl.Element(half), pl.Element(D_MODEL)),
                             lambda i, j, lo_ref, hi_ref: (hi_ref[j] * ROW_UNIT, 0)),
            ],
            out_specs=[
                pl.BlockSpec((tm, tn), lambda i, j, lo_ref, hi_ref: (i, proj_col(j))),
                pl.BlockSpec((tm, half), lambda i, j, lo_ref, hi_ref: (i, glu_col(j))),
                pl.BlockSpec((tm, LANES), lambda i, j, lo_ref, hi_ref: (i, 0)),
            ],
            scratch_shapes=[pltpu.VMEM((tm, D_MODEL), BF16)],
        ),
        out_shape=[
            jax.ShapeDtypeStruct((t, N_PROJ), BF16),
            jax.ShapeDtypeStruct((t, D_MODEL), BF16),
            jax.ShapeDtypeStruct((t, LANES), BF16),
        ],
        compiler_params=pltpu.CompilerParams(
            dimension_semantics=("arbitrary", "arbitrary"),
            vmem_limit_bytes=VMEM_LIMIT),
        name="in_proj",
    )(jnp.asarray(np.array(lo, np.int32) // ROW_UNIT), jnp.asarray(np.array(hi, np.int32) // ROW_UNIT),
      h, g, w_z, wt, wt)


def _cum_matrix(n, reverse):
    ri = lax.broadcasted_iota(jnp.int32, (n, n), 0)
    ci = lax.broadcasted_iota(jnp.int32, (n, n), 1)
    tri = jnp.where((ci >= ri) if reverse else (ci <= ri), 1.0, 0.0).astype(BF16)
    return jnp.concatenate([tri, tri], axis=1)


def _log_decay(z, wa, ba):
    lin = jnp.dot(z, wa, preferred_element_type=F32) + ba
    return (jnp.minimum(lin, 0.0) - jnp.log(1.0 + jnp.exp(-jnp.abs(lin)))) * (1.0 / TAU)


def _cum_decay(la, cum):
    la_hi = la.astype(BF16)
    la_lo = (la - la_hi.astype(F32)).astype(BF16)
    return jnp.dot(cum, jnp.concatenate([la_hi, la_lo], axis=0), preferred_element_type=F32)


def _gla_front_kernel(k_ref, v_ref, z_ref, wa_ref, ba_ref, s_ref):
    la = _log_decay(z_ref[...], wa_ref[...], ba_ref[...])
    row_id = lax.broadcasted_iota(jnp.int32, la.shape, 0)
    la = jnp.where(row_id >= FRONT, la, 0.0)
    b = _cum_decay(la, _cum_matrix(CHUNK, False))
    k_e = (k_ref[...].astype(F32) * jnp.exp(b[CHUNK - 1:CHUNK, :] - b)).astype(BF16)
    for h in range(HEADS):
        s_ref[h] = lax.dot_general(k_e[:, h * DK:(h + 1) * DK], v_ref[:, h * DV:(h + 1) * DV], _TN,
                                   preferred_element_type=F32)


def _gla_front(proj_front, z_front, wa, ba):
    return pl.pallas_call(
        _gla_front_kernel,
        grid=(1,),
        in_specs=[
            pl.BlockSpec((CHUNK, DQK), lambda i: (0, 1)),
            pl.BlockSpec((CHUNK, DV_TOT), lambda i: (0, 1)),
            pl.BlockSpec((CHUNK, LANES), lambda i: (0, 0)),
            pl.BlockSpec((LANES, DQK), lambda i: (0, 0)),
            pl.BlockSpec((1, DQK), lambda i: (0, 0)),
        ],
        out_specs=pl.BlockSpec((HEADS, DK, DV), lambda i: (0, 0, 0)),
        out_shape=jax.ShapeDtypeStruct((HEADS, DK, DV), F32),
        name="gla_front",
    )(proj_front, proj_front, z_front, wa, ba)


def _gla_kernel(blk_ref, first_ref, q_ref, k_ref, v_ref, z_ref, wa_ref, ba_ref, *rest, reverse):
    if reverse:
        of_ref, r_ref, gg_ref, o_ref, s_ref = rest
    else:
        s0_ref, o_ref, s_ref = rest
    t = pl.program_id(0)
    n = GLA_CHUNK
    chunks = range(GLA_BLOCK_CHUNKS)
    heads = range(HEADS)
    ks = [slice(h * DK, (h + 1) * DK) for h in heads]
    vs = [slice(h * DV, (h + 1) * DV) for h in heads]

    @pl.when(first_ref[t] == 1)
    def _():
        s_ref[...] = jnp.zeros_like(s_ref) if reverse else s0_ref[...]

    ri = lax.broadcasted_iota(jnp.int32, (n, n), 0)
    ci = lax.broadcasted_iota(jnp.int32, (n, n), 1)
    score_mask = (ci > ri) if reverse else (ci <= ri)
    cum = _cum_matrix(n, reverse)
    mid_row = n // 2 if reverse else n // 2 - 1
    tot_row = 0 if reverse else n - 1

    la = _log_decay(z_ref[...], wa_ref[...], ba_ref[...])
    b = [_cum_decay(la[c * n:(c + 1) * n, :], cum) for c in chunks]

    q_in, k_in, q_st, k_st, dec = [], [], [], [], []
    for c in chunks:
        sl = pl.ds(c * n, n)
        b_mid = b[c][mid_row:mid_row + 1, :]
        b_tot = b[c][tot_row:tot_row + 1, :]
        q = q_ref[sl, :].astype(F32) * (DK ** -0.5)
        k = k_ref[sl, :].astype(F32)
        q_in.append((q * jnp.exp(b[c] - b_mid)).astype(BF16))
        k_in.append((k * jnp.exp(b_mid - b[c])).astype(BF16))
        q_st.append((q * jnp.exp(b[c])).astype(BF16))
        k_st.append((k * jnp.exp(b_tot - b[c])).astype(BF16))
        dec.append(jnp.broadcast_to(jnp.exp(b_tot), (LANES, DQK)))

    v = [[v_ref[pl.ds(c * n, n), vs[h]] for h in heads] for c in chunks]
    scores = [[lax.dot_general(q_in[c][:, ks[h]], k_in[c][:, ks[h]], _NT, preferred_element_type=F32)
               for h in heads] for c in chunks]
    probs = [[jnp.where(score_mask, scores[c][h], 0.0).astype(BF16) for h in heads] for c in chunks]
    dec_t = [[dec[c][:, ks[h]].T for h in heads] for c in chunks]

    order = range(GLA_BLOCK_CHUNKS - 1, -1, -1) if reverse else chunks
    for c in order:
        sl = pl.ds(c * n, n)
        state = [s_ref[h] for h in heads]
        o_all = [jnp.dot(probs[c][h], v[c][h], preferred_element_type=F32)
                 + jnp.dot(q_st[c][:, ks[h]], state[h].astype(BF16), preferred_element_type=F32) for h in heads]
        for h in heads:
            upd = lax.dot_general(k_st[c][:, ks[h]], v[c][h], _TN, preferred_element_type=F32)
            s_ref[h] = state[h] * jnp.concatenate([dec_t[c][h]] * (DV // LANES), axis=1) + upd
        for h in heads:
            o = o_all[h]
            if reverse:
                o = _rms(o + of_ref[sl, vs[h]], gg_ref[:, vs[h]])
                o_ref[sl, vs[h]] = (o * r_ref[sl, vs[h]].astype(F32)).astype(o_ref.dtype)
            else:
                o_ref[sl, vs[h]] = o


def _gla(proj, z, wa, ba, tables, *, reverse, s_front=None, o_fwd=None, g_gla=None):
    t = proj.shape[0]
    blk, first = tables
    rows = lambda width, col: pl.BlockSpec((GLA_ROWS, width), lambda s, blk_ref, first_ref: (blk_ref[s], col))
    const = lambda shape: pl.BlockSpec(shape, lambda s, blk_ref, first_ref: (0,) * len(shape))
    in_specs = [rows(DQK, 0), rows(DQK, 1), rows(DV_TOT, 1), rows(LANES, 0), const((LANES, DQK)), const((1, DQK))]
    args = [proj, proj, proj, z, wa, ba]
    if reverse:
        in_specs += [rows(DV_TOT, 0), rows(DV_TOT, COL_R), const((1, DV_TOT))]
        args += [o_fwd, proj, g_gla]
    else:
        in_specs += [const((HEADS, DK, DV))]
        args += [s_front]
    return pl.pallas_call(
        functools.partial(_gla_kernel, reverse=reverse),
        grid_spec=pltpu.PrefetchScalarGridSpec(
            num_scalar_prefetch=2,
            grid=(blk.shape[0],),
            in_specs=in_specs,
            out_specs=rows(DV_TOT, 0),
            scratch_shapes=[pltpu.VMEM((HEADS, DK, DV), F32)],
        ),
        out_shape=jax.ShapeDtypeStruct((t, DV_TOT), BF16 if reverse else F32),
        compiler_params=pltpu.CompilerParams(
            dimension_semantics=("arbitrary",),
            vmem_limit_bytes=VMEM_LIMIT),
        name="gla_bwd" if reverse else "gla_fwd",
    )(blk, first, *args)


def _gla_tables(n_seq, seq_len, reverse):
    nb = seq_len // GLA_ROWS
    order = np.arange(nb - 1, -1, -1) if reverse else np.arange(nb)
    blk = np.concatenate([s * nb + order for s in range(n_seq)]).astype(np.int32)
    first = np.tile(np.arange(nb) == 0, n_seq).astype(np.int32)
    return jnp.asarray(blk), jnp.asarray(first)


def _mix_kernel(og_ref, ga_ref, gb_ref, h_ref, g_ref, gp_ref, gn_ref, gf_ref,
                wgo_ref, wco_ref, wout_ref, wdw_ref, bdw_ref, lng_ref, lnb_ref, bco_ref, gpost_ref, gpre_ref,
                h1_ref, u2_ref, g_scr, y_scr, yc_scr, m_scr, *, tm, rb, tiles_per_seq, n_tiles):
    i = pl.program_id(0)
    pos = jnp.minimum(i, n_tiles - 1) % tiles_per_seq
    seq_start = pos == 0
    seq_end = pos == tiles_per_seq - 1

    @pl.when(i == 0)
    def _():
        yc_scr[...] = jnp.zeros_like(yc_scr)

    g_prev = jnp.where(seq_start, gf_ref[...], gp_ref[...]).astype(F32)
    g_next = jnp.where(seq_end, 0.0, gn_ref[...].astype(F32))
    g_cur = g_ref[...].astype(F32)
    for c in range(N_CBLK):
        cs = slice(c * LANES, (c + 1) * LANES)
        g_scr[c, 0:HALO, :] = g_prev[:, cs]
        g_scr[c, HALO:HALO + tm, :] = g_cur[:, cs]
        g_scr[c, HALO + tm:, :] = g_next[:, cs]

    def slab(s, carry):
        for cc in range(CBLK_PER_SLAB):
            c = s * CBLK_PER_SLAB + cc
            for r0 in range(0, tm, rb):
                acc = jnp.zeros((rb, LANES), F32)
                for w in range(CONV_W):
                    start = HALO - CONV_PAD + w + r0
                    acc = acc + g_scr[c, start:start + rb, :] * wdw_ref[c, w:w + 1, :]
                y_scr[c, r0:r0 + rb, :] = acc + bdw_ref[c]
        cols = pl.ds(pl.multiple_of(s * MXU_DIM, MXU_DIM), MXU_DIM)
        y_a = jnp.dot(og_ref[...], wgo_ref[s], preferred_element_type=F32)
        y_b = jnp.dot(yc_scr[...], wco_ref[s], preferred_element_type=F32) + bco_ref[s]
        merged = ga_ref[:, cols].astype(F32) * y_a + gb_ref[:, cols].astype(F32) * y_b
        m_scr[s] = merged.astype(BF16)
        return carry

    lax.fori_loop(0, N_SLAB, slab, 0)

    merged = jnp.concatenate([m_scr[s] for s in range(N_SLAB)], axis=1)
    mix = jnp.dot(merged, wout_ref[...], preferred_element_type=F32)
    h1 = h_ref[...] + _rms(mix, gpost_ref[...])
    h1_ref[...] = h1
    u2_ref[...] = _rms(h1, gpre_ref[...]).astype(BF16)

    s1 = jnp.zeros((tm, LANES), F32)
    for c in range(N_CBLK):
        s1 = s1 + y_scr[c]
    mu = jnp.sum(s1, axis=-1, keepdims=True) / D_MODEL
    s2 = jnp.zeros((tm, LANES), F32)
    for c in range(N_CBLK):
        d = y_scr[c] - mu
        s2 = s2 + d * d
    inv = lax.rsqrt(jnp.sum(s2, axis=-1, keepdims=True) / D_MODEL + EPS)
    for c in range(N_CBLK):
        yn = (y_scr[c] - mu) * inv * lng_ref[c] + lnb_ref[c]
        yc_scr[:, c * LANES:(c + 1) * LANES] = (yn * _sigmoid(yn)).astype(BF16)


def _mix(og, proj, glu, glu_front, h, seq_len, wgo, wco, wout, wdw, bdw, lng, lnb, bco, gpost, gpre, tm, rb):
    t = h.shape[0]
    n_tiles = t // tm
    hb = tm // HALO
    last_hblk = t // HALO - 1
    conv_tile = lambda i: jnp.minimum(i, n_tiles - 1)
    proj_tile = lambda i: jnp.maximum(i - 1, 0)
    ptile = lambda col: pl.BlockSpec((tm, D_MODEL), lambda i: (proj_tile(i), col))
    ctile = pl.BlockSpec((tm, D_MODEL), lambda i: (conv_tile(i), 0))
    prev = pl.BlockSpec((HALO, D_MODEL), lambda i: (jnp.maximum(conv_tile(i) * hb - 1, 0), 0))
    nxt = pl.BlockSpec((HALO, D_MODEL), lambda i: (jnp.minimum((conv_tile(i) + 1) * hb, last_hblk), 0))
    meta = pl.BlockSpec((HALO, D_MODEL), lambda i: (FRONT // HALO, 0))
    resident = lambda shape: pl.BlockSpec(shape, lambda i: (0,) * len(shape), pipeline_mode=pl.Buffered(1))
    vec = resident((1, D_MODEL))
    cvec = resident((N_CBLK, 1, LANES))
    return pl.pallas_call(
        functools.partial(_mix_kernel, tm=tm, rb=rb, tiles_per_seq=seq_len // tm, n_tiles=n_tiles),
        grid=(n_tiles + 1,),
        in_specs=[
            ptile(0), ptile(COL_GATE), ptile(COL_GATE + 1), ptile(0),
            ctile, prev, nxt, meta,
            resident((N_SLAB, DV_TOT, MXU_DIM)), resident((N_SLAB, D_MODEL, MXU_DIM)), resident((D_MODEL, D_MODEL)),
            resident((N_CBLK, CONV_W + 1, LANES)), cvec, cvec, cvec, resident((N_SLAB, 1, MXU_DIM)), vec, vec,
        ],
        out_specs=[ptile(0), ptile(0)],
        out_shape=[jax.ShapeDtypeStruct((t, D_MODEL), F32), jax.ShapeDtypeStruct((t, D_MODEL), BF16)],
        scratch_shapes=[
            pltpu.VMEM((N_CBLK, tm + 2 * HALO, LANES), F32),
            pltpu.VMEM((N_CBLK, tm, LANES), F32),
            pltpu.VMEM((tm, D_MODEL), BF16),
            pltpu.VMEM((N_SLAB, tm, MXU_DIM), BF16),
        ],
        compiler_params=pltpu.CompilerParams(
            dimension_semantics=("arbitrary",),
            vmem_limit_bytes=VMEM_LIMIT),
        name="mix",
    )(og, proj, proj, h, glu, glu, glu, glu_front,
      wgo, wco, wout, wdw, bdw, lng, lnb, bco, gpost, gpre)


def _mlp_kernel(u_ref, h_ref, wu_ref, wd_ref, g_ref, o_ref):
    f = pl.program_id(1)

    @pl.when(f == 0)
    def _():
        o_ref[...] = jnp.zeros_like(o_ref)

    up = jnp.dot(u_ref[...], wu_ref[...], preferred_element_type=F32)
    act = jnp.square(jnp.maximum(up, 0.0)).astype(BF16)
    o_ref[...] += jnp.dot(act, wd_ref[...], preferred_element_type=F32)

    @pl.when(f == pl.num_programs(1) - 1)
    def _():
        o_ref[...] = h_ref[...] + _rms(o_ref[...], g_ref[...])


def _mlp(u2, h1, w_up, w_down, g, tm, tf):
    t = u2.shape[0]
    return pl.pallas_call(
        _mlp_kernel,
        grid=(t // tm, D_FF // tf),
        in_specs=[
            pl.BlockSpec((tm, D_MODEL), lambda i, f: (i, 0)),
            pl.BlockSpec((tm, D_MODEL), lambda i, f: (i, 0)),
            pl.BlockSpec((D_MODEL, tf), lambda i, f: (0, f)),
            pl.BlockSpec((tf, D_MODEL), lambda i, f: (f, 0)),
            pl.BlockSpec((1, D_MODEL), lambda i, f: (0, 0)),
        ],
        out_specs=pl.BlockSpec((tm, D_MODEL), lambda i, f: (i, 0)),
        out_shape=jax.ShapeDtypeStruct((t, D_MODEL), F32),
        compiler_params=pltpu.CompilerParams(
            dimension_semantics=("arbitrary", "arbitrary"),
            vmem_limit_bytes=VMEM_LIMIT),
        name="mlp",
    )(u2, h1, w_up, w_down, g)


def _cblocks(vec):
    return vec.astype(F32).reshape(N_CBLK, 1, LANES)


def _col_slabs(w):
    return w.reshape(w.shape[0], N_SLAB, MXU_DIM).transpose(1, 0, 2).astype(BF16)


def kernel(x_prompt, x_sample, meta_tokens, g_pre_mix, w_in, w_a2_f, b_a_f, w_a2_b, b_a_b, g_gla,
           w_gla_o, w_dw, b_dw, ln_g, ln_b, w_conv_o, b_conv_o, w_out, g_post_mix, g_pre_mlp,
           w_up, w_down, g_post_mlp):
    assert w_in.shape[0] == 1
    l = 0
    tm_proj, tn_proj, tm_mix, rb_mix, tm_mlp, tf_mlp = 1024, 2048, 256, 64, 512, 1024
    row = lambda v: v.astype(F32).reshape(1, -1)
    z0 = N_HEAD_COLS

    w_main = w_in[l].T.astype(BF16)
    w_z = jnp.pad(w_in[l].T[z0:z0 + 2 * GATE_RANK], ((0, LANES - 2 * GATE_RANK), (0, 0))).astype(BF16)
    wa_f = jnp.pad(w_a2_f[l], ((0, LANES - GATE_RANK), (0, 0))).astype(BF16)
    wa_b = jnp.pad(w_a2_b[l], ((GATE_RANK, LANES - 2 * GATE_RANK), (0, 0))).astype(BF16)
    wdw = jnp.pad(w_dw[l].astype(F32), ((0, 1), (0, 0))).reshape(CONV_W + 1, N_CBLK, LANES).transpose(1, 0, 2)
    w_go, w_co, w_o = _col_slabs(w_gla_o[l]), _col_slabs(w_conv_o[l]), w_out[l].astype(BF16)
    b_co = b_conv_o[l].astype(F32).reshape(N_SLAB, 1, MXU_DIM)
    w_u, w_d = w_up[l].astype(BF16), w_down[l].astype(BF16)

    front = jnp.concatenate([jnp.zeros((FRONT, D_MODEL), F32), meta_tokens.astype(F32)], axis=0)
    proj_front, glu_front, z_front = _in_proj(front, row(g_pre_mix[l]), w_main, w_z, CHUNK, tn_proj)
    s_front = _gla_front(proj_front, z_front, wa_f, row(b_a_f[l]))

    outs = []
    for x in (x_prompt, x_sample):
        bsz, seq_len, _ = x.shape
        assert seq_len % GLA_ROWS == 0 and seq_len % tm_mix == 0
        h = x.reshape(bsz * seq_len, D_MODEL)
        proj, glu, z = _in_proj(h, row(g_pre_mix[l]), w_main, w_z, tm_proj, tn_proj)
        o_f = _gla(proj, z, wa_f, row(b_a_f[l]), _gla_tables(bsz, seq_len, False), reverse=False, s_front=s_front)
        og = _gla(proj, z, wa_b, row(b_a_b[l]), _gla_tables(bsz, seq_len, True), reverse=True,
                  o_fwd=o_f, g_gla=row(g_gla[l]))
        h1, u2 = _mix(og, proj, glu, glu_front, h, seq_len, w_go, w_co, w_o, wdw, _cblocks(b_dw[l]),
                      _cblocks(ln_g[l]), _cblocks(ln_b[l]), b_co, row(g_post_mix[l]),
                      row(g_pre_mlp[l]), tm_mix, rb_mix)
        y = _mlp(u2, h1, w_u, w_d, row(g_post_mlp[l]), tm_mlp, tf_mlp)
        outs.append(y.reshape(bsz, seq_len, D_MODEL))
    return tuple(outs)
```

```python
import functools

import jax
import jax.numpy as jnp
import numpy as np
from jax import lax
from jax.experimental import pallas as pl
from jax.experimental.pallas import tpu as pltpu

F32 = jnp.float32
BF16 = jnp.bfloat16

D_MODEL = 2048
N_META = 16
HEADS = 4
DK = 256
DV = 512
DQK = HEADS * DK
DV_TOT = HEADS * DV
GATE_RANK = 16
TAU = 16.0
CHUNK = 64
FRONT = CHUNK - N_META
CONV_W = 31
CONV_PAD = CONV_W // 2
D_FF = 4 * D_MODEL
EPS = 1e-6

LANES = 128
MXU_DIM = 256
HALO = 16
N_CBLK = D_MODEL // LANES
N_SLAB = D_MODEL // MXU_DIM
CBLK_PER_SLAB = MXU_DIM // LANES
N_HEAD_COLS = 2 * DQK + 2 * DV_TOT
N_MAIN = N_HEAD_COLS + 4 * D_MODEL
N_PROJ = N_HEAD_COLS + 2 * D_MODEL
COL_R = (2 * DQK + DV_TOT) // D_MODEL
COL_GATE = N_HEAD_COLS // D_MODEL
ROW_UNIT = 2 * GATE_RANK
PIECE_ROWS = 512

GLA_CHUNK = 2 * CHUNK
GLA_BLOCK_CHUNKS = 4
GLA_ROWS = GLA_BLOCK_CHUNKS * GLA_CHUNK

VMEM_LIMIT = 58 * 1024 * 1024

_NT = (((1,), (1,)), ((), ()))
_TN = (((0,), (0,)), ((), ()))


def _sigmoid(x):
    return 1.0 / (1.0 + jnp.exp(-x))


def _rms(x, g):
    ms = jnp.mean(x * x, axis=-1, keepdims=True)
    return x * lax.rsqrt(ms + EPS) * g


def _in_proj_kernel(lo_ref, hi_ref, x_ref, g_ref, wz_ref, wlo_ref, whi_ref, o_ref, glu_ref, z_ref, u_ref, *, tiles):
    n_plain, n_swish, n_glu = tiles
    j = pl.program_id(1)
    half = wlo_ref.shape[0]
    tm = u_ref.shape[0]
    rows_per_piece = min(tm, PIECE_ROWS)

    @pl.when(j == 0)
    def _():
        u = _rms(x_ref[...], g_ref[...]).astype(BF16)
        u_ref[...] = u
        z_ref[...] = lax.dot_general(u, wz_ref[...], _NT, preferred_element_type=F32).astype(BF16)

    is_glu = jnp.logical_and(j >= n_plain + n_swish, j < n_plain + n_swish + n_glu)

    def project(activation):
        for w_ref, base in ((wlo_ref, 0), (whi_ref, half)):
            for lo in range(0, half, MXU_DIM):
                for r0 in range(0, tm, rows_per_piece):
                    rs = slice(r0, r0 + rows_per_piece)
                    acc = lax.dot_general(u_ref[rs, :], w_ref[lo:lo + MXU_DIM, :], _NT, preferred_element_type=F32)
                    o_ref[rs, base + lo:base + lo + MXU_DIM] = activation(acc).astype(BF16)

    pl.when(j < n_plain)(lambda: project(lambda a: a))
    pl.when(jnp.logical_and(j >= n_plain, j < n_plain + n_swish))(lambda: project(lambda a: a * _sigmoid(a)))
    pl.when(j >= n_plain + n_swish + n_glu)(lambda: project(_sigmoid))

    @pl.when(is_glu)
    def _():
        for lo in range(0, half, MXU_DIM):
            cols = slice(lo, lo + MXU_DIM)
            for r0 in range(0, tm, rows_per_piece):
                rs = slice(r0, r0 + rows_per_piece)
                val = lax.dot_general(u_ref[rs, :], wlo_ref[cols, :], _NT, preferred_element_type=F32)
                gate = lax.dot_general(u_ref[rs, :], whi_ref[cols, :], _NT, preferred_element_type=F32)
                glu_ref[rs, cols] = (val * _sigmoid(gate)).astype(BF16)


def _in_proj(h, g, wt, w_z, tm, tn):
    t = h.shape[0]
    half = tn // 2
    assert (2 * DQK + DV_TOT) % tn == 0 and DV_TOT % tn == 0
    n_plain, n_swish, n_glu = (2 * DQK + DV_TOT) // tn, DV_TOT // tn, 2 * D_MODEL // tn
    first_glu = n_plain + n_swish
    z1 = N_HEAD_COLS + 2 * GATE_RANK
    lo = [j * tn for j in range(first_glu)] + [z1 + c * half for c in range(n_glu)]
    hi = [j * tn + half for j in range(first_glu)] + [z1 + D_MODEL + c * half for c in range(n_glu)]
    for s in range(z1 + 2 * D_MODEL, wt.shape[0], tn):
        lo.append(s)
        hi.append(s + half)
    n_tiles = len(lo)
    assert n_tiles * tn == N_MAIN and all(s % ROW_UNIT == 0 for s in lo + hi)
    proj_col = lambda j: jnp.where(j < first_glu, j, jnp.maximum(j - n_glu, first_glu - 1))
    glu_col = lambda j: jnp.clip(j - first_glu, 0, n_glu - 1)
    return pl.pallas_call(
        functools.partial(_in_proj_kernel, tiles=(n_plain, n_swish, n_glu)),
        grid_spec=pltpu.PrefetchScalarGridSpec(
            num_scalar_prefetch=2,
            grid=(t // tm, n_tiles),
            in_specs=[
                pl.BlockSpec((tm, D_MODEL),
                             lambda i, j, lo_ref, hi_ref: (jnp.minimum(i + (j > n_tiles // 2), t // tm - 1), 0)),
                pl.BlockSpec((1, D_MODEL), lambda i, j, lo_ref, hi_ref: (0, 0)),
                pl.BlockSpec((LANES, D_MODEL), lambda i, j, lo_ref, hi_ref: (0, 0)),
                pl.BlockSpec((pl.Element(half), pl.Element(D_MODEL)),
                             lambda i, j, lo_ref, hi_ref: (lo_ref[j] * ROW_UNIT, 0)),
                pl.BlockSpec((pl.Element(half), pl.Element(D_MODEL)),
                             lambda i, j, lo_ref, hi_ref: (hi_ref[j] * ROW_UNIT, 0)),
            ],
            out_specs=[
                pl.BlockSpec((tm, tn), lambda i, j, lo_ref, hi_ref: (i, proj_col(j))),
                pl.BlockSpec((tm, half), lambda i, j, lo_ref, hi_ref: (i, glu_col(j))),
                pl.BlockSpec((tm, LANES), lambda i, j, lo_ref, hi_ref: (i, 0)),
            ],
            scratch_shapes=[pltpu.VMEM((tm, D_MODEL), BF16)],
        ),
        out_shape=[
            jax.ShapeDtypeStruct((t, N_PROJ), BF16),
            jax.ShapeDtypeStruct((t, D_MODEL), BF16),
            jax.ShapeDtypeStruct((t, LANES), BF16),
        ],
        compiler_params=pltpu.CompilerParams(
            dimension_semantics=("arbitrary", "arbitrary"),
            vmem_limit_bytes=VMEM_LIMIT),
        name="in_proj",
    )(jnp.asarray(np.array(lo, np.int32) // ROW_UNIT), jnp.asarray(np.array(hi, np.int32) // ROW_UNIT),
      h, g, w_z, wt, wt)


def _cum_matrix(n, reverse):
    ri = lax.broadcasted_iota(jnp.int32, (n, n), 0)
    ci = lax.broadcasted_iota(jnp.int32, (n, n), 1)
    tri = jnp.where((ci >= ri) if reverse else (ci <= ri), 1.0, 0.0).astype(BF16)
    return jnp.concatenate([tri, tri], axis=1)


def _log_decay(z, wa, ba):
    lin = jnp.dot(z, wa, preferred_element_type=F32) + ba
    return (jnp.minimum(lin, 0.0) - jnp.log(1.0 + jnp.exp(-jnp.abs(lin)))) * (1.0 / TAU)


def _cum_decay(la, cum):
    la_hi = la.astype(BF16)
    la_lo = (la - la_hi.astype(F32)).astype(BF16)
    return jnp.dot(cum, jnp.concatenate([la_hi, la_lo], axis=0), preferred_element_type=F32)


def _gla_front_kernel(k_ref, v_ref, z_ref, wa_ref, ba_ref, s_ref):
    la = _log_decay(z_ref[...], wa_ref[...], ba_ref[...])
    row_id = lax.broadcasted_iota(jnp.int32, la.shape, 0)
    la = jnp.where(row_id >= FRONT, la, 0.0)
    b = _cum_decay(la, _cum_matrix(CHUNK, False))
    k_e = (k_ref[...].astype(F32) * jnp.exp(b[CHUNK - 1:CHUNK, :] - b)).astype(BF16)
    for h in range(HEADS):
        s_ref[h] = lax.dot_general(k_e[:, h * DK:(h + 1) * DK], v_ref[:, h * DV:(h + 1) * DV], _TN,
                                   preferred_element_type=F32)


def _gla_front(proj_front, z_front, wa, ba):
    return pl.pallas_call(
        _gla_front_kernel,
        grid=(1,),
        in_specs=[
            pl.BlockSpec((CHUNK, DQK), lambda i: (0, 1)),
            pl.BlockSpec((CHUNK, DV_TOT), lambda i: (0, 1)),
            pl.BlockSpec((CHUNK, LANES), lambda i: (0, 0)),
            pl.BlockSpec((LANES, DQK), lambda i: (0, 0)),
            pl.BlockSpec((1, DQK), lambda i: (0, 0)),
        ],
        out_specs=pl.BlockSpec((HEADS, DK, DV), lambda i: (0, 0, 0)),
        out_shape=jax.ShapeDtypeStruct((HEADS, DK, DV), F32),
        name="gla_front",
    )(proj_front, proj_front, z_front, wa, ba)


def _gla_kernel(blk_ref, first_ref, q_ref, k_ref, v_ref, z_ref, wa_ref, ba_ref, *rest, reverse):
    if reverse:
        of_ref, r_ref, gg_ref, o_ref, s_ref = rest
    else:
        s0_ref, o_ref, s_ref = rest
    t = pl.program_id(0)
    n = GLA_CHUNK
    chunks = range(GLA_BLOCK_CHUNKS)
    heads = range(HEADS)
    ks = [slice(h * DK, (h + 1) * DK) for h in heads]
    vs = [slice(h * DV, (h + 1) * DV) for h in heads]

    @pl.when(first_ref[t] == 1)
    def _():
        s_ref[...] = jnp.zeros_like(s_ref) if reverse else s0_ref[...]

    ri = lax.broadcasted_iota(jnp.int32, (n, n), 0)
    ci = lax.broadcasted_iota(jnp.int32, (n, n), 1)
    score_mask = (ci > ri) if reverse else (ci <= ri)
    cum = _cum_matrix(n, reverse)
    mid_row = n // 2 if reverse else n // 2 - 1
    tot_row = 0 if reverse else n - 1

    la = _log_decay(z_ref[...], wa_ref[...], ba_ref[...])
    b = [_cum_decay(la[c * n:(c + 1) * n, :], cum) for c in chunks]

    q_in, k_in, q_st, k_st, dec = [], [], [], [], []
    for c in chunks:
        sl = pl.ds(c * n, n)
        b_mid = b[c][mid_row:mid_row + 1, :]
        b_tot = b[c][tot_row:tot_row + 1, :]
        q = q_ref[sl, :].astype(F32) * (DK ** -0.5)
        k = k_ref[sl, :].astype(F32)
        q_in.append((q * jnp.exp(b[c] - b_mid)).astype(BF16))
        k_in.append((k * jnp.exp(b_mid - b[c])).astype(BF16))
        q_st.append((q * jnp.exp(b[c])).astype(BF16))
        k_st.append((k * jnp.exp(b_tot - b[c])).astype(BF16))
        dec.append(jnp.broadcast_to(jnp.exp(b_tot), (LANES, DQK)))

    v = [[v_ref[pl.ds(c * n, n), vs[h]] for h in heads] for c in chunks]
    scores = [[lax.dot_general(q_in[c][:, ks[h]], k_in[c][:, ks[h]], _NT, preferred_element_type=F32)
               for h in heads] for c in chunks]
    probs = [[jnp.where(score_mask, scores[c][h], 0.0).astype(BF16) for h in heads] for c in chunks]
    dec_t = [[dec[c][:, ks[h]].T for h in heads] for c in chunks]

    order = range(GLA_BLOCK_CHUNKS - 1, -1, -1) if reverse else chunks
    for c in order:
        sl = pl.ds(c * n, n)
        state = [s_ref[h] for h in heads]
        o_all = [jnp.dot(probs[c][h], v[c][h], preferred_element_type=F32)
                 + jnp.dot(q_st[c][:, ks[h]], state[h].astype(BF16), preferred_element_type=F32) for h in heads]
        for h in heads:
            upd = lax.dot_general(k_st[c][:, ks[h]], v[c][h], _TN, preferred_element_type=F32)
            s_ref[h] = state[h] * jnp.concatenate([dec_t[c][h]] * (DV // LANES), axis=1) + upd
        for h in heads:
            o = o_all[h]
            if reverse:
                o = _rms(o + of_ref[sl, vs[h]], gg_ref[:, vs[h]])
                o_ref[sl, vs[h]] = (o * r_ref[sl, vs[h]].astype(F32)).astype(o_ref.dtype)
            else:
                o_ref[sl, vs[h]] = o


def _gla(proj, z, wa, ba, tables, *, reverse, s_front=None, o_fwd=None, g_gla=None):
    t = proj.shape[0]
    blk, first = tables
    rows = lambda width, col: pl.BlockSpec((GLA_ROWS, width), lambda s, blk_ref, first_ref: (blk_ref[s], col))
    const = lambda shape: pl.BlockSpec(shape, lambda s, blk_ref, first_ref: (0,) * len(shape))
    in_specs = [rows(DQK, 0), rows(DQK, 1), rows(DV_TOT, 1), rows(LANES, 0), const((LANES, DQK)), const((1, DQK))]
    args = [proj, proj, proj, z, wa, ba]
    if reverse:
        in_specs += [rows(DV_TOT, 0), rows(DV_TOT, COL_R), const((1, DV_TOT))]
        args += [o_fwd, proj, g_gla]
    else:
        in_specs += [const((HEADS, DK, DV))]
        args += [s_front]
    return pl.pallas_call(
        functools.partial(_gla_kernel, reverse=reverse),
        grid_spec=pltpu.PrefetchScalarGridSpec(
            num_scalar_prefetch=2,
            grid=(blk.shape[0],),
            in_specs=in_specs,
            out_specs=rows(DV_TOT, 0),
            scratch_shapes=[pltpu.VMEM((HEADS, DK, DV), F32)],
        ),
        out_shape=jax.ShapeDtypeStruct((t, DV_TOT), BF16 if reverse else F32),
        compiler_params=pltpu.CompilerParams(
            dimension_semantics=("arbitrary",),
            vmem_limit_bytes=VMEM_LIMIT),
        name="gla_bwd" if reverse else "gla_fwd",
    )(blk, first, *args)


def _gla_tables(n_seq, seq_len, reverse):
    nb = seq_len // GLA_ROWS
    order = np.arange(nb - 1, -1, -1) if reverse else np.arange(nb)
    blk = np.concatenate([s * nb + order for s in range(n_seq)]).astype(np.int32)
    first = np.tile(np.arange(nb) == 0, n_seq).astype(np.int32)
    return jnp.asarray(blk), jnp.asarray(first)


def _mix_kernel(og_ref, ga_ref, gb_ref, h_ref, g_ref, gp_ref, gn_ref, gf_ref,
                wgo_ref, wco_ref, wout_ref, wdw_ref, bdw_ref, lng_ref, lnb_ref, bco_ref, gpost_ref, gpre_ref,
                h1_ref, u2_ref, g_scr, y_scr, yc_scr, m_scr, *, tm, rb, tiles_per_seq, n_tiles):
    i = pl.program_id(0)
    pos = jnp.minimum(i, n_tiles - 1) % tiles_per_seq
    seq_start = pos == 0
    seq_end = pos == tiles_per_seq - 1

    @pl.when(i == 0)
    def _():
        yc_scr[...] = jnp.zeros_like(yc_scr)

    g_prev = jnp.where(seq_start, gf_ref[...], gp_ref[...]).astype(F32)
    g_next = jnp.where(seq_end, 0.0, gn_ref[...].astype(F32))
    g_cur = g_ref[...].astype(F32)
    for c in range(N_CBLK):
        cs = slice(c * LANES, (c + 1) * LANES)
        g_scr[c, 0:HALO, :] = g_prev[:, cs]
        g_scr[c, HALO:HALO + tm, :] = g_cur[:, cs]
        g_scr[c, HALO + tm:, :] = g_next[:, cs]

    def slab(s, carry):
        for cc in range(CBLK_PER_SLAB):
            c = s * CBLK_PER_SLAB + cc
            for r0 in range(0, tm, rb):
                acc = jnp.zeros((rb, LANES), F32)
                for w in range(CONV_W):
                    start = HALO - CONV_PAD + w + r0
                    acc = acc + g_scr[c, start:start + rb, :] * wdw_ref[c, w:w + 1, :]
                y_scr[c, r0:r0 + rb, :] = acc + bdw_ref[c]
        cols = pl.ds(pl.multiple_of(s * MXU_DIM, MXU_DIM), MXU_DIM)
        y_a = jnp.dot(og_ref[...], wgo_ref[s], preferred_element_type=F32)
        y_b = jnp.dot(yc_scr[...], wco_ref[s], preferred_element_type=F32) + bco_ref[s]
        merged = ga_ref[:, cols].astype(F32) * y_a + gb_ref[:, cols].astype(F32) * y_b
        m_scr[s] = merged.astype(BF16)
        return carry

    lax.fori_loop(0, N_SLAB, slab, 0)

    merged = jnp.concatenate([m_scr[s] for s in range(N_SLAB)], axis=1)
    mix = jnp.dot(merged, wout_ref[...], preferred_element_type=F32)
    h1 = h_ref[...] + _rms(mix, gpost_ref[...])
    h1_ref[...] = h1
    u2_ref[...] = _rms(h1, gpre_ref[...]).astype(BF16)

    s1 = jnp.zeros((tm, LANES), F32)
    for c in range(N_CBLK):
        s1 = s1 + y_scr[c]
    mu = jnp.sum(s1, axis=-1, keepdims=True) / D_MODEL
    s2 = jnp.zeros((tm, LANES), F32)
    for c in range(N_CBLK):
        d = y_scr[c] - mu
        s2 = s2 + d * d
    inv = lax.rsqrt(jnp.sum(s2, axis=-1, keepdims=True) / D_MODEL + EPS)
    for c in range(N_CBLK):
        yn = (y_scr[c] - mu) * inv * lng_ref[c] + lnb_ref[c]
        yc_scr[:, c * LANES:(c + 1) * LANES] = (yn * _sigmoid(yn)).astype(BF16)


def _mix(og, proj, glu, glu_front, h, seq_len, wgo, wco, wout, wdw, bdw, lng, lnb, bco, gpost, gpre, tm, rb):
    t = h.shape[0]
    n_tiles = t // tm
    hb = tm // HALO
    last_hblk = t // HALO - 1
    conv_tile = lambda i: jnp.minimum(i, n_tiles - 1)
    proj_tile = lambda i: jnp.maximum(i - 1, 0)
    ptile = lambda col: pl.BlockSpec((tm, D_MODEL), lambda i: (proj_tile(i), col))
    ctile = pl.BlockSpec((tm, D_MODEL), lambda i: (conv_tile(i), 0))
    prev = pl.BlockSpec((HALO, D_MODEL), lambda i: (jnp.maximum(conv_tile(i) * hb - 1, 0), 0))
    nxt = pl.BlockSpec((HALO, D_MODEL), lambda i: (jnp.minimum((conv_tile(i) + 1) * hb, last_hblk), 0))
    meta = pl.BlockSpec((HALO, D_MODEL), lambda i: (FRONT // HALO, 0))
    resident = lambda shape: pl.BlockSpec(shape, lambda i: (0,) * len(shape), pipeline_mode=pl.Buffered(1))
    vec = resident((1, D_MODEL))
    cvec = resident((N_CBLK, 1, LANES))
    return pl.pallas_call(
        functools.partial(_mix_kernel, tm=tm, rb=rb, tiles_per_seq=seq_len // tm, n_tiles=n_tiles),
        grid=(n_tiles + 1,),
        in_specs=[
            ptile(0), ptile(COL_GATE), ptile(COL_GATE + 1), ptile(0),
            ctile, prev, nxt, meta,
            resident((N_SLAB, DV_TOT, MXU_DIM)), resident((N_SLAB, D_MODEL, MXU_DIM)), resident((D_MODEL, D_MODEL)),
            resident((N_CBLK, CONV_W + 1, LANES)), cvec, cvec, cvec, resident((N_SLAB, 1, MXU_DIM)), vec, vec,
        ],
        out_specs=[ptile(0), ptile(0)],
        out_shape=[jax.ShapeDtypeStruct((t, D_MODEL), F32), jax.ShapeDtypeStruct((t, D_MODEL), BF16)],
        scratch_shapes=[
            pltpu.VMEM((N_CBLK, tm + 2 * HALO, LANES), F32),
            pltpu.VMEM((N_CBLK, tm, LANES), F32),
            pltpu.VMEM((tm, D_MODEL), BF16),
            pltpu.VMEM((N_SLAB, tm, MXU_DIM), BF16),
        ],
        compiler_params=pltpu.CompilerParams(
            dimension_semantics=("arbitrary",),
            vmem_limit_bytes=VMEM_LIMIT),
        name="mix",
    )(og, proj, proj, h, glu, glu, glu, glu_front,
      wgo, wco, wout, wdw, bdw, lng, lnb, bco, gpost, gpre)


def _mlp_kernel(u_ref, h_ref, wu_ref, wd_ref, g_ref, o_ref):
    f = pl.program_id(1)

    @pl.when(f == 0)
    def _():
        o_ref[...] = jnp.zeros_like(o_ref)

    up = jnp.dot(u_ref[...], wu_ref[...], preferred_element_type=F32)
    act = jnp.square(jnp.maximum(up, 0.0)).astype(BF16)
    o_ref[...] += jnp.dot(act, wd_ref[...], preferred_element_type=F32)

    @pl.when(f == pl.num_programs(1) - 1)
    def _():
        o_ref[...] = h_ref[...] + _rms(o_ref[...], g_ref[...])


def _mlp(u2, h1, w_up, w_down, g, tm, tf):
    t = u2.shape[0]
    assert D_FF // tf > 1
    return pl.pallas_call(
        _mlp_kernel,
        grid=(t // tm, D_FF // tf),
        in_specs=[
            pl.BlockSpec((tm, D_MODEL), lambda i, f: (i, 0)),
            pl.BlockSpec((tm, D_MODEL), lambda i, f: (jnp.maximum(i - (f == 0), 0), 0)),
            pl.BlockSpec((D_MODEL, tf), lambda i, f: (0, f)),
            pl.BlockSpec((tf, D_MODEL), lambda i, f: (f, 0)),
            pl.BlockSpec((1, D_MODEL), lambda i, f: (0, 0)),
        ],
        out_specs=pl.BlockSpec((tm, D_MODEL), lambda i, f: (i, 0)),
        out_shape=jax.ShapeDtypeStruct((t, D_MODEL), F32),
        compiler_params=pltpu.CompilerParams(
            dimension_semantics=("arbitrary", "arbitrary"),
            vmem_limit_bytes=VMEM_LIMIT),
        name="mlp",
    )(u2, h1, w_up, w_down, g)


def _cblocks(vec):
    return vec.astype(F32).reshape(N_CBLK, 1, LANES)


def _col_slabs(w):
    return w.reshape(w.shape[0], N_SLAB, MXU_DIM).transpose(1, 0, 2).astype(BF16)


def kernel(x_prompt, x_sample, meta_tokens, g_pre_mix, w_in, w_a2_f, b_a_f, w_a2_b, b_a_b, g_gla,
           w_gla_o, w_dw, b_dw, ln_g, ln_b, w_conv_o, b_conv_o, w_out, g_post_mix, g_pre_mlp,
           w_up, w_down, g_post_mlp):
    assert w_in.shape[0] == 1
    l = 0
    tm_proj, tn_proj, tm_mix, rb_mix, tm_mlp, tf_mlp = 1024, 2048, 256, 64, 512, 1024
    row = lambda v: v.astype(F32).reshape(1, -1)
    z0 = N_HEAD_COLS

    w_main = w_in[l].T.astype(BF16)
    w_z = jnp.pad(w_in[l].T[z0:z0 + 2 * GATE_RANK], ((0, LANES - 2 * GATE_RANK), (0, 0))).astype(BF16)
    wa_f = jnp.pad(w_a2_f[l], ((0, LANES - GATE_RANK), (0, 0))).astype(BF16)
    wa_b = jnp.pad(w_a2_b[l], ((GATE_RANK, LANES - 2 * GATE_RANK), (0, 0))).astype(BF16)
    wdw = jnp.pad(w_dw[l].astype(F32), ((0, 1), (0, 0))).reshape(CONV_W + 1, N_CBLK, LANES).transpose(1, 0, 2)
    w_go, w_co, w_o = _col_slabs(w_gla_o[l]), _col_slabs(w_conv_o[l]), w_out[l].astype(BF16)
    b_co = b_conv_o[l].astype(F32).reshape(N_SLAB, 1, MXU_DIM)
    w_u, w_d = w_up[l].astype(BF16), w_down[l].astype(BF16)

    front = jnp.concatenate([jnp.zeros((FRONT, D_MODEL), F32), meta_tokens.astype(F32)], axis=0)
    proj_front, glu_front, z_front = _in_proj(front, row(g_pre_mix[l]), w_main, w_z, CHUNK, tn_proj)
    s_front = _gla_front(proj_front, z_front, wa_f, row(b_a_f[l]))

    outs = []
    for x in (x_prompt, x_sample):
        bsz, seq_len, _ = x.shape
        assert seq_len % GLA_ROWS == 0 and seq_len % tm_mix == 0
        h = x.reshape(bsz * seq_len, D_MODEL)
        proj, glu, z = _in_proj(h, row(g_pre_mix[l]), w_main, w_z, tm_proj, tn_proj)
        o_f = _gla(proj, z, wa_f, row(b_a_f[l]), _gla_tables(bsz, seq_len, False), reverse=False, s_front=s_front)
        og = _gla(proj, z, wa_b, row(b_a_b[l]), _gla_tables(bsz, seq_len, True), reverse=True,
                  o_fwd=o_f, g_gla=row(g_gla[l]))
        h1, u2 = _mix(og, proj, glu, glu_front, h, seq_len, w_go, w_co, w_o, wdw, _cblocks(b_dw[l]),
                      _cblocks(ln_g[l]), _cblocks(ln_b[l]), b_co, row(g_post_mix[l]),
                      row(g_pre_mlp[l]), tm_mix, rb_mix)
        y = _mlp(u2, h1, w_u, w_d, row(g_post_mlp[l]), tm_mlp, tf_mlp)
        outs.append(y.reshape(bsz, seq_len, D_MODEL))
    return tuple(outs)
```

```python
import functools

import jax
import jax.numpy as jnp
import numpy as np
from jax import lax
from jax.experimental import pallas as pl
from jax.experimental.pallas import tpu as pltpu

F32 = jnp.float32
BF16 = jnp.bfloat16

D_MODEL = 2048
N_META = 16
HEADS = 4
DK = 256
DV = 512
DQK = HEADS * DK
DV_TOT = HEADS * DV
GATE_RANK = 16
TAU = 16.0
CHUNK = 64
FRONT = CHUNK - N_META
CONV_W = 31
CONV_PAD = CONV_W // 2
D_FF = 4 * D_MODEL
EPS = 1e-6

LANES = 128
MXU_DIM = 256
HALO = 16
N_CBLK = D_MODEL // LANES
N_SLAB = D_MODEL // MXU_DIM
CBLK_PER_SLAB = MXU_DIM // LANES
N_HEAD_COLS = 2 * DQK + 2 * DV_TOT
N_MAIN = N_HEAD_COLS + 4 * D_MODEL
N_PROJ = N_HEAD_COLS + 2 * D_MODEL
COL_R = (2 * DQK + DV_TOT) // D_MODEL
COL_GATE = N_HEAD_COLS // D_MODEL
ROW_UNIT = 2 * GATE_RANK
PIECE_ROWS = 512

GLA_CHUNK = 2 * CHUNK
GLA_BLOCK_CHUNKS = 4
GLA_ROWS = GLA_BLOCK_CHUNKS * GLA_CHUNK

VMEM_LIMIT = 58 * 1024 * 1024

_NT = (((1,), (1,)), ((), ()))
_TN = (((0,), (0,)), ((), ()))


def _sigmoid(x):
    return 1.0 / (1.0 + jnp.exp(-x))


def _rms(x, g):
    ms = jnp.mean(x * x, axis=-1, keepdims=True)
    return x * lax.rsqrt(ms + EPS) * g


def _in_proj_kernel(lo_ref, hi_ref, x_ref, g_ref, wz_ref, wlo_ref, whi_ref, o_ref, glu_ref, z_ref, u_ref, *, tiles):
    n_plain, n_swish, n_glu = tiles
    j = pl.program_id(1)
    half = wlo_ref.shape[0]
    tm = u_ref.shape[0]
    rows_per_piece = min(tm, PIECE_ROWS)

    @pl.when(j == 0)
    def _():
        u = _rms(x_ref[...], g_ref[...]).astype(BF16)
        u_ref[...] = u
        z_ref[...] = lax.dot_general(u, wz_ref[...], _NT, preferred_element_type=F32).astype(BF16)

    is_glu = jnp.logical_and(j >= n_plain + n_swish, j < n_plain + n_swish + n_glu)

    def project(activation):
        for w_ref, base in ((wlo_ref, 0), (whi_ref, half)):
            for lo in range(0, half, MXU_DIM):
                for r0 in range(0, tm, rows_per_piece):
                    rs = slice(r0, r0 + rows_per_piece)
                    acc = lax.dot_general(u_ref[rs, :], w_ref[lo:lo + MXU_DIM, :], _NT, preferred_element_type=F32)
                    o_ref[rs, base + lo:base + lo + MXU_DIM] = activation(acc).astype(BF16)

    pl.when(j < n_plain)(lambda: project(lambda a: a))
    pl.when(jnp.logical_and(j >= n_plain, j < n_plain + n_swish))(lambda: project(lambda a: a * _sigmoid(a)))
    pl.when(j >= n_plain + n_swish + n_glu)(lambda: project(_sigmoid))

    @pl.when(is_glu)
    def _():
        for lo in range(0, half, MXU_DIM):
            cols = slice(lo, lo + MXU_DIM)
            for r0 in range(0, tm, rows_per_piece):
                rs = slice(r0, r0 + rows_per_piece)
                val = lax.dot_general(u_ref[rs, :], wlo_ref[cols, :], _NT, preferred_element_type=F32)
                gate = lax.dot_general(u_ref[rs, :], whi_ref[cols, :], _NT, preferred_element_type=F32)
                glu_ref[rs, cols] = (val * _sigmoid(gate)).astype(BF16)


def _in_proj(h, g, wt, w_z, tm, tn):
    t = h.shape[0]
    half = tn // 2
    assert (2 * DQK + DV_TOT) % tn == 0 and DV_TOT % tn == 0
    n_plain, n_swish, n_glu = (2 * DQK + DV_TOT) // tn, DV_TOT // tn, 2 * D_MODEL // tn
    first_glu = n_plain + n_swish
    z1 = N_HEAD_COLS + 2 * GATE_RANK
    lo = [j * tn for j in range(first_glu)] + [z1 + c * half for c in range(n_glu)]
    hi = [j * tn + half for j in range(first_glu)] + [z1 + D_MODEL + c * half for c in range(n_glu)]
    for s in range(z1 + 2 * D_MODEL, wt.shape[0], tn):
        lo.append(s)
        hi.append(s + half)
    n_tiles = len(lo)
    assert n_tiles * tn == N_MAIN and all(s % ROW_UNIT == 0 for s in lo + hi)
    proj_col = lambda j: jnp.where(j < first_glu, j, jnp.maximum(j - n_glu, first_glu - 1))
    glu_col = lambda j: jnp.clip(j - first_glu, 0, n_glu - 1)
    return pl.pallas_call(
        functools.partial(_in_proj_kernel, tiles=(n_plain, n_swish, n_glu)),
        grid_spec=pltpu.PrefetchScalarGridSpec(
            num_scalar_prefetch=2,
            grid=(t // tm, n_tiles),
            in_specs=[
                pl.BlockSpec((tm, D_MODEL), lambda i, j, lo_ref, hi_ref: (i, 0)),
                pl.BlockSpec((1, D_MODEL), lambda i, j, lo_ref, hi_ref: (0, 0)),
                pl.BlockSpec((LANES, D_MODEL), lambda i, j, lo_ref, hi_ref: (0, 0)),
                pl.BlockSpec((pl.Element(half), pl.Element(D_MODEL)),
                             lambda i, j, lo_ref, hi_ref: (lo_ref[j] * ROW_UNIT, 0)),
                pl.BlockSpec((pl.Element(half), pl.Element(D_MODEL)),
                             lambda i, j, lo_ref, hi_ref: (hi_ref[j] * ROW_UNIT, 0)),
            ],
            out_specs=[
                pl.BlockSpec((tm, tn), lambda i, j, lo_ref, hi_ref: (i, proj_col(j))),
                pl.BlockSpec((tm, half), lambda i, j, lo_ref, hi_ref: (i, glu_col(j))),
                pl.BlockSpec((tm, LANES), lambda i, j, lo_ref, hi_ref: (i, 0)),
            ],
            scratch_shapes=[pltpu.VMEM((tm, D_MODEL), BF16)],
        ),
        out_shape=[
            jax.ShapeDtypeStruct((t, N_PROJ), BF16),
            jax.ShapeDtypeStruct((t, D_MODEL), BF16),
            jax.ShapeDtypeStruct((t, LANES), BF16),
        ],
        compiler_params=pltpu.CompilerParams(
            dimension_semantics=("arbitrary", "arbitrary"),
            vmem_limit_bytes=VMEM_LIMIT),
        name="in_proj",
    )(jnp.asarray(np.array(lo, np.int32) // ROW_UNIT), jnp.asarray(np.array(hi, np.int32) // ROW_UNIT),
      h, g, w_z, wt, wt)


def _cum_matrix(n, reverse):
    ri = lax.broadcasted_iota(jnp.int32, (n, n), 0)
    ci = lax.broadcasted_iota(jnp.int32, (n, n), 1)
    tri = jnp.where((ci >= ri) if reverse else (ci <= ri), 1.0, 0.0).astype(BF16)
    return jnp.concatenate([tri, tri], axis=1)


def _log_decay(z, wa, ba):
    lin = jnp.dot(z, wa, preferred_element_type=F32) + ba
    return (jnp.minimum(lin, 0.0) - jnp.log(1.0 + jnp.exp(-jnp.abs(lin)))) * (1.0 / TAU)


def _cum_decay(la, cum):
    la_hi = la.astype(BF16)
    la_lo = (la - la_hi.astype(F32)).astype(BF16)
    return jnp.dot(cum, jnp.concatenate([la_hi, la_lo], axis=0), preferred_element_type=F32)


def _gla_front_kernel(k_ref, v_ref, z_ref, wa_ref, ba_ref, s_ref):
    la = _log_decay(z_ref[...], wa_ref[...], ba_ref[...])
    row_id = lax.broadcasted_iota(jnp.int32, la.shape, 0)
    la = jnp.where(row_id >= FRONT, la, 0.0)
    b = _cum_decay(la, _cum_matrix(CHUNK, False))
    k_e = (k_ref[...].astype(F32) * jnp.exp(b[CHUNK - 1:CHUNK, :] - b)).astype(BF16)
    for h in range(HEADS):
        s_ref[h] = lax.dot_general(k_e[:, h * DK:(h + 1) * DK], v_ref[:, h * DV:(h + 1) * DV], _TN,
                                   preferred_element_type=F32)


def _gla_front(proj_front, z_front, wa, ba):
    return pl.pallas_call(
        _gla_front_kernel,
        grid=(1,),
        in_specs=[
            pl.BlockSpec((CHUNK, DQK), lambda i: (0, 1)),
            pl.BlockSpec((CHUNK, DV_TOT), lambda i: (0, 1)),
            pl.BlockSpec((CHUNK, LANES), lambda i: (0, 0)),
            pl.BlockSpec((LANES, DQK), lambda i: (0, 0)),
            pl.BlockSpec((1, DQK), lambda i: (0, 0)),
        ],
        out_specs=pl.BlockSpec((HEADS, DK, DV), lambda i: (0, 0, 0)),
        out_shape=jax.ShapeDtypeStruct((HEADS, DK, DV), F32),
        name="gla_front",
    )(proj_front, proj_front, z_front, wa, ba)


def _gla_kernel(blk_ref, first_ref, q_ref, k_ref, v_ref, z_ref, wa_ref, ba_ref, *rest, reverse):
    if reverse:
        of_ref, r_ref, gg_ref, o_ref, s_ref = rest
    else:
        s0_ref, o_ref, s_ref = rest
    t = pl.program_id(0)
    n = GLA_CHUNK
    chunks = range(GLA_BLOCK_CHUNKS)
    heads = range(HEADS)
    ks = [slice(h * DK, (h + 1) * DK) for h in heads]
    vs = [slice(h * DV, (h + 1) * DV) for h in heads]

    @pl.when(first_ref[t] == 1)
    def _():
        s_ref[...] = jnp.zeros_like(s_ref) if reverse else s0_ref[...]

    ri = lax.broadcasted_iota(jnp.int32, (n, n), 0)
    ci = lax.broadcasted_iota(jnp.int32, (n, n), 1)
    score_mask = (ci > ri) if reverse else (ci <= ri)
    cum = _cum_matrix(n, reverse)
    mid_row = n // 2 if reverse else n // 2 - 1
    tot_row = 0 if reverse else n - 1

    la = _log_decay(z_ref[...], wa_ref[...], ba_ref[...])
    b = [_cum_decay(la[c * n:(c + 1) * n, :], cum) for c in chunks]

    q_in, k_in, q_st, k_st, dec = [], [], [], [], []
    for c in chunks:
        sl = pl.ds(c * n, n)
        b_mid = b[c][mid_row:mid_row + 1, :]
        b_tot = b[c][tot_row:tot_row + 1, :]
        q = q_ref[sl, :].astype(F32) * (DK ** -0.5)
        k = k_ref[sl, :].astype(F32)
        q_in.append((q * jnp.exp(b[c] - b_mid)).astype(BF16))
        k_in.append((k * jnp.exp(b_mid - b[c])).astype(BF16))
        q_st.append((q * jnp.exp(b[c])).astype(BF16))
        k_st.append((k * jnp.exp(b_tot - b[c])).astype(BF16))
        dec.append(jnp.broadcast_to(jnp.exp(b_tot), (LANES, DQK)))

    v = [[v_ref[pl.ds(c * n, n), vs[h]] for h in heads] for c in chunks]
    scores = [[lax.dot_general(q_in[c][:, ks[h]], k_in[c][:, ks[h]], _NT, preferred_element_type=F32)
               for h in heads] for c in chunks]
    probs = [[jnp.where(score_mask, scores[c][h], 0.0).astype(BF16) for h in heads] for c in chunks]
    dec_t = [[dec[c][:, ks[h]].T for h in heads] for c in chunks]

    order = range(GLA_BLOCK_CHUNKS - 1, -1, -1) if reverse else chunks
    for c in order:
        sl = pl.ds(c * n, n)
        state = [s_ref[h] for h in heads]
        o_all = [jnp.dot(probs[c][h], v[c][h], preferred_element_type=F32)
                 + jnp.dot(q_st[c][:, ks[h]], state[h].astype(BF16), preferred_element_type=F32) for h in heads]
        for h in heads:
            upd = lax.dot_general(k_st[c][:, ks[h]], v[c][h], _TN, preferred_element_type=F32)
            s_ref[h] = state[h] * jnp.concatenate([dec_t[c][h]] * (DV // LANES), axis=1) + upd
        for h in heads:
            o = o_all[h]
            if reverse:
                o = _rms(o + of_ref[sl, vs[h]], gg_ref[:, vs[h]])
                o_ref[sl, vs[h]] = (o * r_ref[sl, vs[h]].astype(F32)).astype(o_ref.dtype)
            else:
                o_ref[sl, vs[h]] = o


def _gla(proj, z, wa, ba, tables, *, reverse, s_front=None, o_fwd=None, g_gla=None):
    t = proj.shape[0]
    blk, first = tables
    rows = lambda width, col: pl.BlockSpec((GLA_ROWS, width), lambda s, blk_ref, first_ref: (blk_ref[s], col))
    const = lambda shape: pl.BlockSpec(shape, lambda s, blk_ref, first_ref: (0,) * len(shape))
    in_specs = [rows(DQK, 0), rows(DQK, 1), rows(DV_TOT, 1), rows(LANES, 0), const((LANES, DQK)), const((1, DQK))]
    args = [proj, proj, proj, z, wa, ba]
    if reverse:
        in_specs += [rows(DV_TOT, 0), rows(DV_TOT, COL_R), const((1, DV_TOT))]
        args += [o_fwd, proj, g_gla]
    else:
        in_specs += [const((HEADS, DK, DV))]
        args += [s_front]
    return pl.pallas_call(
        functools.partial(_gla_kernel, reverse=reverse),
        grid_spec=pltpu.PrefetchScalarGridSpec(
            num_scalar_prefetch=2,
            grid=(blk.shape[0],),
            in_specs=in_specs,
            out_specs=rows(DV_TOT, 0),
            scratch_shapes=[pltpu.VMEM((HEADS, DK, DV), F32)],
        ),
        out_shape=jax.ShapeDtypeStruct((t, DV_TOT), BF16 if reverse else F32),
        compiler_params=pltpu.CompilerParams(
            dimension_semantics=("arbitrary",),
            vmem_limit_bytes=VMEM_LIMIT),
        name="gla_bwd" if reverse else "gla_fwd",
    )(blk, first, *args)


def _gla_tables(n_seq, seq_len, reverse):
    nb = seq_len // GLA_ROWS
    order = np.arange(nb - 1, -1, -1) if reverse else np.arange(nb)
    blk = np.concatenate([s * nb + order for s in range(n_seq)]).astype(np.int32)
    first = np.tile(np.arange(nb) == 0, n_seq).astype(np.int32)
    return jnp.asarray(blk), jnp.asarray(first)


def _mix_kernel(og_ref, ga_ref, gb_ref, h_ref, g_ref, gp_ref, gn_ref, gf_ref,
                wgo_ref, wco_ref, wout_ref, wdw_ref, bdw_ref, lng_ref, lnb_ref, bco_ref, gpost_ref, gpre_ref,
                h1_ref, u2_ref, g_scr, y_scr, yc_scr, m_scr, *, tm, rb, tiles_per_seq, n_tiles):
    i = pl.program_id(0)
    pos = jnp.minimum(i, n_tiles - 1) % tiles_per_seq
    seq_start = pos == 0
    seq_end = pos == tiles_per_seq - 1

    @pl.when(i == 0)
    def _():
        yc_scr[...] = jnp.zeros_like(yc_scr)

    g_prev = jnp.where(seq_start, gf_ref[...], gp_ref[...]).astype(F32)
    g_next = jnp.where(seq_end, 0.0, gn_ref[...].astype(F32))
    g_cur = g_ref[...].astype(F32)
    for c in range(N_CBLK):
        cs = slice(c * LANES, (c + 1) * LANES)
        g_scr[c, 0:HALO, :] = g_prev[:, cs]
        g_scr[c, HALO:HALO + tm, :] = g_cur[:, cs]
        g_scr[c, HALO + tm:, :] = g_next[:, cs]

    def slab(s, carry):
        for cc in range(CBLK_PER_SLAB):
            c = s * CBLK_PER_SLAB + cc
            for r0 in range(0, tm, rb):
                acc = jnp.zeros((rb, LANES), F32)
                for w in range(CONV_W):
                    start = HALO - CONV_PAD + w + r0
                    acc = acc + g_scr[c, start:start + rb, :] * wdw_ref[c, w:w + 1, :]
                y_scr[c, r0:r0 + rb, :] = acc + bdw_ref[c]
        cols = pl.ds(pl.multiple_of(s * MXU_DIM, MXU_DIM), MXU_DIM)
        y_a = jnp.dot(og_ref[...], wgo_ref[s], preferred_element_type=F32)
        y_b = jnp.dot(yc_scr[...], wco_ref[s], preferred_element_type=F32) + bco_ref[s]
        merged = ga_ref[:, cols].astype(F32) * y_a + gb_ref[:, cols].astype(F32) * y_b
        m_scr[s] = merged.astype(BF16)
        return carry

    lax.fori_loop(0, N_SLAB, slab, 0)

    merged = jnp.concatenate([m_scr[s] for s in range(N_SLAB)], axis=1)
    mix = jnp.dot(merged, wout_ref[...], preferred_element_type=F32)
    h1 = h_ref[...] + _rms(mix, gpost_ref[...])
    h1_ref[...] = h1
    u2_ref[...] = _rms(h1, gpre_ref[...]).astype(BF16)

    s1 = jnp.zeros((tm, LANES), F32)
    for c in range(N_CBLK):
        s1 = s1 + y_scr[c]
    mu = jnp.sum(s1, axis=-1, keepdims=True) / D_MODEL
    s2 = jnp.zeros((tm, LANES), F32)
    for c in range(N_CBLK):
        d = y_scr[c] - mu
        s2 = s2 + d * d
    inv = lax.rsqrt(jnp.sum(s2, axis=-1, keepdims=True) / D_MODEL + EPS)
    for c in range(N_CBLK):
        yn = (y_scr[c] - mu) * inv * lng_ref[c] + lnb_ref[c]
        yc_scr[:, c * LANES:(c + 1) * LANES] = (yn * _sigmoid(yn)).astype(BF16)


def _mix(og, proj, glu, glu_front, h, seq_len, wgo, wco, wout, wdw, bdw, lng, lnb, bco, gpost, gpre, tm, rb):
    t = h.shape[0]
    n_tiles = t // tm
    hb = tm // HALO
    last_hblk = t // HALO - 1
    conv_tile = lambda i: jnp.minimum(i, n_tiles - 1)
    proj_tile = lambda i: jnp.maximum(i - 1, 0)
    ptile = lambda col: pl.BlockSpec((tm, D_MODEL), lambda i: (proj_tile(i), col))
    ctile = pl.BlockSpec((tm, D_MODEL), lambda i: (conv_tile(i), 0))
    prev = pl.BlockSpec((HALO, D_MODEL), lambda i: (jnp.maximum(conv_tile(i) * hb - 1, 0), 0))
    nxt = pl.BlockSpec((HALO, D_MODEL), lambda i: (jnp.minimum((conv_tile(i) + 1) * hb, last_hblk), 0))
    meta = pl.BlockSpec((HALO, D_MODEL), lambda i: (FRONT // HALO, 0))
    resident = lambda shape: pl.BlockSpec(shape, lambda i: (0,) * len(shape), pipeline_mode=pl.Buffered(1))
    vec = resident((1, D_MODEL))
    cvec = resident((N_CBLK, 1, LANES))
    return pl.pallas_call(
        functools.partial(_mix_kernel, tm=tm, rb=rb, tiles_per_seq=seq_len // tm, n_tiles=n_tiles),
        grid=(n_tiles + 1,),
        in_specs=[
            ptile(0), ptile(COL_GATE), ptile(COL_GATE + 1), ptile(0),
            ctile, prev, nxt, meta,
            resident((N_SLAB, DV_TOT, MXU_DIM)), resident((N_SLAB, D_MODEL, MXU_DIM)), resident((D_MODEL, D_MODEL)),
            resident((N_CBLK, CONV_W + 1, LANES)), cvec, cvec, cvec, resident((N_SLAB, 1, MXU_DIM)), vec, vec,
        ],
        out_specs=[ptile(0), ptile(0)],
        out_shape=[jax.ShapeDtypeStruct((t, D_MODEL), F32), jax.ShapeDtypeStruct((t, D_MODEL), BF16)],
        scratch_shapes=[
            pltpu.VMEM((N_CBLK, tm + 2 * HALO, LANES), F32),
            pltpu.VMEM((N_CBLK, tm, LANES), F32),
            pltpu.VMEM((tm, D_MODEL), BF16),
            pltpu.VMEM((N_SLAB, tm, MXU_DIM), BF16),
        ],
        compiler_params=pltpu.CompilerParams(
            dimension_semantics=("arbitrary",),
            vmem_limit_bytes=VMEM_LIMIT),
        name="mix",
    )(og, proj, proj, h, glu, glu, glu, glu_front,
      wgo, wco, wout, wdw, bdw, lng, lnb, bco, gpost, gpre)


def _mlp_kernel(u_ref, h_ref, wu_ref, wd_ref, g_ref, o_ref):
    f = pl.program_id(1)

    @pl.when(f == 0)
    def _():
        o_ref[...] = jnp.zeros_like(o_ref)

    up = jnp.dot(u_ref[...], wu_ref[...], preferred_element_type=F32)
    act = jnp.square(jnp.maximum(up, 0.0)).astype(BF16)
    o_ref[...] += jnp.dot(act, wd_ref[...], preferred_element_type=F32)

    @pl.when(f == pl.num_programs(1) - 1)
    def _():
        o_ref[...] = h_ref[...] + _rms(o_ref[...], g_ref[...])


def _mlp(u2, h1, w_up, w_down, g, tm, tf):
    t = u2.shape[0]
    return pl.pallas_call(
        _mlp_kernel,
        grid=(t // tm, D_FF // tf),
        in_specs=[
            pl.BlockSpec((tm, D_MODEL), lambda i, f: (i, 0)),
            pl.BlockSpec((tm, D_MODEL), lambda i, f: (i, 0)),
            pl.BlockSpec((D_MODEL, tf), lambda i, f: (0, f)),
            pl.BlockSpec((tf, D_MODEL), lambda i, f: (f, 0)),
            pl.BlockSpec((1, D_MODEL), lambda i, f: (0, 0)),
        ],
        out_specs=pl.BlockSpec((tm, D_MODEL), lambda i, f: (i, 0)),
        out_shape=jax.ShapeDtypeStruct((t, D_MODEL), F32),
        compiler_params=pltpu.CompilerParams(
            dimension_semantics=("arbitrary", "arbitrary"),
            vmem_limit_bytes=VMEM_LIMIT),
        name="mlp",
    )(u2, h1, w_up, w_down, g)


def _cblocks(vec):
    return vec.astype(F32).reshape(N_CBLK, 1, LANES)


def _col_slabs(w):
    return w.reshape(w.shape[0], N_SLAB, MXU_DIM).transpose(1, 0, 2).astype(BF16)


def kernel(x_prompt, x_sample, meta_tokens, g_pre_mix, w_in, w_a2_f, b_a_f, w_a2_b, b_a_b, g_gla,
           w_gla_o, w_dw, b_dw, ln_g, ln_b, w_conv_o, b_conv_o, w_out, g_post_mix, g_pre_mlp,
           w_up, w_down, g_post_mlp):
    assert w_in.shape[0] == 1
    l = 0
    tm_proj, tn_proj, tm_mix, rb_mix, tm_mlp, tf_mlp = 512, 2048, 256, 64, 512, 1024
    row = lambda v: v.astype(F32).reshape(1, -1)
    z0 = N_HEAD_COLS

    w_main = w_in[l].T.astype(BF16)
    w_z = jnp.pad(w_in[l].T[z0:z0 + 2 * GATE_RANK], ((0, LANES - 2 * GATE_RANK), (0, 0))).astype(BF16)
    wa_f = jnp.pad(w_a2_f[l], ((0, LANES - GATE_RANK), (0, 0))).astype(BF16)
    wa_b = jnp.pad(w_a2_b[l], ((GATE_RANK, LANES - 2 * GATE_RANK), (0, 0))).astype(BF16)
    wdw = jnp.pad(w_dw[l].astype(F32), ((0, 1), (0, 0))).reshape(CONV_W + 1, N_CBLK, LANES).transpose(1, 0, 2)
    w_go, w_co, w_o = _col_slabs(w_gla_o[l]), _col_slabs(w_conv_o[l]), w_out[l].astype(BF16)
    b_co = b_conv_o[l].astype(F32).reshape(N_SLAB, 1, MXU_DIM)
    w_u, w_d = w_up[l].astype(BF16), w_down[l].astype(BF16)

    front = jnp.concatenate([jnp.zeros((FRONT, D_MODEL), F32), meta_tokens.astype(F32)], axis=0)
    proj_front, glu_front, z_front = _in_proj(front, row(g_pre_mix[l]), w_main, w_z, CHUNK, tn_proj)
    s_front = _gla_front(proj_front, z_front, wa_f, row(b_a_f[l]))

    outs = []
    for x in (x_prompt, x_sample):
        bsz, seq_len, _ = x.shape
        assert seq_len % GLA_ROWS == 0 and seq_len % tm_mix == 0
        h = x.reshape(bsz * seq_len, D_MODEL)
        proj, glu, z = _in_proj(h, row(g_pre_mix[l]), w_main, w_z, tm_proj, tn_proj)
        o_f = _gla(proj, z, wa_f, row(b_a_f[l]), _gla_tables(bsz, seq_len, False), reverse=False, s_front=s_front)
        og = _gla(proj, z, wa_b, row(b_a_b[l]), _gla_tables(bsz, seq_len, True), reverse=True,
                  o_fwd=o_f, g_gla=row(g_gla[l]))
        h1, u2 = _mix(og, proj, glu, glu_front, h, seq_len, w_go, w_co, w_o, wdw, _cblocks(b_dw[l]),
                      _cblocks(ln_g[l]), _cblocks(ln_b[l]), b_co, row(g_post_mix[l]),
                      row(g_pre_mlp[l]), tm_mix, rb_mix)
        y = _mlp(u2, h1, w_u, w_d, row(g_post_mlp[l]), tm_mlp, tf_mlp)
        outs.append(y.reshape(bsz, seq_len, D_MODEL))
    return tuple(outs)
```

```python
import functools

import jax
import jax.numpy as jnp
import numpy as np
from jax import lax
from jax.experimental import pallas as pl
from jax.experimental.pallas import tpu as pltpu

F32 = jnp.float32
BF16 = jnp.bfloat16

D_MODEL = 2048
N_META = 16
HEADS = 4
DK = 256
DV = 512
DQK = HEADS * DK
DV_TOT = HEADS * DV
GATE_RANK = 16
TAU = 16.0
CHUNK = 64
FRONT = CHUNK - N_META
CONV_W = 31
CONV_PAD = CONV_W // 2
D_FF = 4 * D_MODEL
EPS = 1e-6

LANES = 128
MXU_DIM = 256
HALO = 16
N_CBLK = D_MODEL // LANES
N_SLAB = D_MODEL // MXU_DIM
CBLK_PER_SLAB = MXU_DIM // LANES
N_HEAD_COLS = 2 * DQK + 2 * DV_TOT
N_MAIN = N_HEAD_COLS + 4 * D_MODEL
N_PROJ = N_HEAD_COLS + 2 * D_MODEL
COL_R = (2 * DQK + DV_TOT) // D_MODEL
COL_GATE = N_HEAD_COLS // D_MODEL
ROW_UNIT = 2 * GATE_RANK
PIECE_ROWS = 256

GLA_CHUNK = 2 * CHUNK
GLA_BLOCK_CHUNKS = 4
GLA_ROWS = GLA_BLOCK_CHUNKS * GLA_CHUNK

VMEM_LIMIT = 58 * 1024 * 1024

_NT = (((1,), (1,)), ((), ()))
_TN = (((0,), (0,)), ((), ()))


def _sigmoid(x):
    return 1.0 / (1.0 + jnp.exp(-x))


def _rms(x, g):
    ms = jnp.mean(x * x, axis=-1, keepdims=True)
    return x * lax.rsqrt(ms + EPS) * g


def _in_proj_kernel(lo_ref, hi_ref, x_ref, g_ref, wz_ref, wlo_ref, whi_ref, o_ref, glu_ref, z_ref, u_ref, *, tiles):
    n_plain, n_swish, n_glu = tiles
    j = pl.program_id(1)
    half = wlo_ref.shape[0]
    tm = u_ref.shape[0]
    rows_per_piece = min(tm, PIECE_ROWS)

    @pl.when(j == 0)
    def _():
        u = _rms(x_ref[...], g_ref[...]).astype(BF16)
        u_ref[...] = u
        z_ref[...] = lax.dot_general(u, wz_ref[...], _NT, preferred_element_type=F32).astype(BF16)

    is_glu = jnp.logical_and(j >= n_plain + n_swish, j < n_plain + n_swish + n_glu)

    def project(activation):
        for w_ref, base in ((wlo_ref, 0), (whi_ref, half)):
            for lo in range(0, half, MXU_DIM):
                for r0 in range(0, tm, rows_per_piece):
                    rs = slice(r0, r0 + rows_per_piece)
                    acc = lax.dot_general(u_ref[rs, :], w_ref[lo:lo + MXU_DIM, :], _NT, preferred_element_type=F32)
                    o_ref[rs, base + lo:base + lo + MXU_DIM] = activation(acc).astype(BF16)

    pl.when(j < n_plain)(lambda: project(lambda a: a))
    pl.when(jnp.logical_and(j >= n_plain, j < n_plain + n_swish))(lambda: project(lambda a: a * _sigmoid(a)))
    pl.when(j >= n_plain + n_swish + n_glu)(lambda: project(_sigmoid))

    @pl.when(is_glu)
    def _():
        for lo in range(0, half, MXU_DIM):
            cols = slice(lo, lo + MXU_DIM)
            for r0 in range(0, tm, rows_per_piece):
                rs = slice(r0, r0 + rows_per_piece)
                val = lax.dot_general(u_ref[rs, :], wlo_ref[cols, :], _NT, preferred_element_type=F32)
                gate = lax.dot_general(u_ref[rs, :], whi_ref[cols, :], _NT, preferred_element_type=F32)
                glu_ref[rs, cols] = (val * _sigmoid(gate)).astype(BF16)


def _in_proj(h, g, wt, w_z, tm, tn):
    t = h.shape[0]
    half = tn // 2
    assert (2 * DQK + DV_TOT) % tn == 0 and DV_TOT % tn == 0
    n_plain, n_swish, n_glu = (2 * DQK + DV_TOT) // tn, DV_TOT // tn, 2 * D_MODEL // tn
    first_glu = n_plain + n_swish
    z1 = N_HEAD_COLS + 2 * GATE_RANK
    lo = [j * tn for j in range(first_glu)] + [z1 + c * half for c in range(n_glu)]
    hi = [j * tn + half for j in range(first_glu)] + [z1 + D_MODEL + c * half for c in range(n_glu)]
    for s in range(z1 + 2 * D_MODEL, wt.shape[0], tn):
        lo.append(s)
        hi.append(s + half)
    n_tiles = len(lo)
    assert n_tiles * tn == N_MAIN and all(s % ROW_UNIT == 0 for s in lo + hi)
    proj_col = lambda j: jnp.where(j < first_glu, j, jnp.maximum(j - n_glu, first_glu - 1))
    glu_col = lambda j: jnp.clip(j - first_glu, 0, n_glu - 1)
    return pl.pallas_call(
        functools.partial(_in_proj_kernel, tiles=(n_plain, n_swish, n_glu)),
        grid_spec=pltpu.PrefetchScalarGridSpec(
            num_scalar_prefetch=2,
            grid=(t // tm, n_tiles),
            in_specs=[
                pl.BlockSpec((tm, D_MODEL), lambda i, j, lo_ref, hi_ref: (i, 0)),
                pl.BlockSpec((1, D_MODEL), lambda i, j, lo_ref, hi_ref: (0, 0)),
                pl.BlockSpec((LANES, D_MODEL), lambda i, j, lo_ref, hi_ref: (0, 0)),
                pl.BlockSpec((pl.Element(half), pl.Element(D_MODEL)),
                             lambda i, j, lo_ref, hi_ref: (lo_ref[j] * ROW_UNIT, 0)),
                pl.BlockSpec((pl.Element(half), pl.Element(D_MODEL)),
                             lambda i, j, lo_ref, hi_ref: (hi_ref[j] * ROW_UNIT, 0)),
            ],
            out_specs=[
                pl.BlockSpec((tm, tn), lambda i, j, lo_ref, hi_ref: (i, proj_col(j))),
                pl.BlockSpec((tm, half), lambda i, j, lo_ref, hi_ref: (i, glu_col(j))),
                pl.BlockSpec((tm, LANES), lambda i, j, lo_ref, hi_ref: (i, 0)),
            ],
            scratch_shapes=[pltpu.VMEM((tm, D_MODEL), BF16)],
        ),
        out_shape=[
            jax.ShapeDtypeStruct((t, N_PROJ), BF16),
            jax.ShapeDtypeStruct((t, D_MODEL), BF16),
            jax.ShapeDtypeStruct((t, LANES), BF16),
        ],
        compiler_params=pltpu.CompilerParams(
            dimension_semantics=("arbitrary", "arbitrary"),
            vmem_limit_bytes=VMEM_LIMIT),
        name="in_proj",
    )(jnp.asarray(np.array(lo, np.int32) // ROW_UNIT), jnp.asarray(np.array(hi, np.int32) // ROW_UNIT),
      h, g, w_z, wt, wt)


def _cum_matrix(n, reverse):
    ri = lax.broadcasted_iota(jnp.int32, (n, n), 0)
    ci = lax.broadcasted_iota(jnp.int32, (n, n), 1)
    tri = jnp.where((ci >= ri) if reverse else (ci <= ri), 1.0, 0.0).astype(BF16)
    return jnp.concatenate([tri, tri], axis=1)


def _log_decay(z, wa, ba):
    lin = jnp.dot(z, wa, preferred_element_type=F32) + ba
    return (jnp.minimum(lin, 0.0) - jnp.log(1.0 + jnp.exp(-jnp.abs(lin)))) * (1.0 / TAU)


def _cum_decay(la, cum):
    la_hi = la.astype(BF16)
    la_lo = (la - la_hi.astype(F32)).astype(BF16)
    return jnp.dot(cum, jnp.concatenate([la_hi, la_lo], axis=0), preferred_element_type=F32)


def _gla_front_kernel(k_ref, v_ref, z_ref, wa_ref, ba_ref, s_ref):
    la = _log_decay(z_ref[...], wa_ref[...], ba_ref[...])
    row_id = lax.broadcasted_iota(jnp.int32, la.shape, 0)
    la = jnp.where(row_id >= FRONT, la, 0.0)
    b = _cum_decay(la, _cum_matrix(CHUNK, False))
    k_e = (k_ref[...].astype(F32) * jnp.exp(b[CHUNK - 1:CHUNK, :] - b)).astype(BF16)
    for h in range(HEADS):
        s_ref[h] = lax.dot_general(k_e[:, h * DK:(h + 1) * DK], v_ref[:, h * DV:(h + 1) * DV], _TN,
                                   preferred_element_type=F32)


def _gla_front(proj_front, z_front, wa, ba):
    return pl.pallas_call(
        _gla_front_kernel,
        grid=(1,),
        in_specs=[
            pl.BlockSpec((CHUNK, DQK), lambda i: (0, 1)),
            pl.BlockSpec((CHUNK, DV_TOT), lambda i: (0, 1)),
            pl.BlockSpec((CHUNK, LANES), lambda i: (0, 0)),
            pl.BlockSpec((LANES, DQK), lambda i: (0, 0)),
            pl.BlockSpec((1, DQK), lambda i: (0, 0)),
        ],
        out_specs=pl.BlockSpec((HEADS, DK, DV), lambda i: (0, 0, 0)),
        out_shape=jax.ShapeDtypeStruct((HEADS, DK, DV), F32),
        name="gla_front",
    )(proj_front, proj_front, z_front, wa, ba)


def _gla_kernel(blk_ref, first_ref, q_ref, k_ref, v_ref, z_ref, wa_ref, ba_ref, *rest, reverse):
    if reverse:
        of_ref, r_ref, gg_ref, o_ref, s_ref = rest
    else:
        s0_ref, o_ref, s_ref = rest
    t = pl.program_id(0)
    n = GLA_CHUNK
    chunks = range(GLA_BLOCK_CHUNKS)
    heads = range(HEADS)
    ks = [slice(h * DK, (h + 1) * DK) for h in heads]
    vs = [slice(h * DV, (h + 1) * DV) for h in heads]

    @pl.when(first_ref[t] == 1)
    def _():
        s_ref[...] = jnp.zeros_like(s_ref) if reverse else s0_ref[...]

    ri = lax.broadcasted_iota(jnp.int32, (n, n), 0)
    ci = lax.broadcasted_iota(jnp.int32, (n, n), 1)
    score_mask = (ci > ri) if reverse else (ci <= ri)
    cum = _cum_matrix(n, reverse)
    mid_row = n // 2 if reverse else n // 2 - 1
    tot_row = 0 if reverse else n - 1

    la = _log_decay(z_ref[...], wa_ref[...], ba_ref[...])
    b = [_cum_decay(la[c * n:(c + 1) * n, :], cum) for c in chunks]

    q_in, k_in, q_st, k_st, dec = [], [], [], [], []
    for c in chunks:
        sl = pl.ds(c * n, n)
        b_mid = b[c][mid_row:mid_row + 1, :]
        b_tot = b[c][tot_row:tot_row + 1, :]
        q = q_ref[sl, :].astype(F32) * (DK ** -0.5)
        k = k_ref[sl, :].astype(F32)
        q_in.append((q * jnp.exp(b[c] - b_mid)).astype(BF16))
        k_in.append((k * jnp.exp(b_mid - b[c])).astype(BF16))
        q_st.append((q * jnp.exp(b[c])).astype(BF16))
        k_st.append((k * jnp.exp(b_tot - b[c])).astype(BF16))
        dec.append(jnp.broadcast_to(jnp.exp(b_tot), (LANES, DQK)))

    v = [[v_ref[pl.ds(c * n, n), vs[h]] for h in heads] for c in chunks]
    scores = [[lax.dot_general(q_in[c][:, ks[h]], k_in[c][:, ks[h]], _NT, preferred_element_type=F32)
               for h in heads] for c in chunks]
    probs = [[jnp.where(score_mask, scores[c][h], 0.0).astype(BF16) for h in heads] for c in chunks]
    dec_t = [[dec[c][:, ks[h]].T for h in heads] for c in chunks]

    order = range(GLA_BLOCK_CHUNKS - 1, -1, -1) if reverse else chunks
    for c in order:
        sl = pl.ds(c * n, n)
        state = [s_ref[h] for h in heads]
        o_all = [jnp.dot(probs[c][h], v[c][h], preferred_element_type=F32)
                 + jnp.dot(q_st[c][:, ks[h]], state[h].astype(BF16), preferred_element_type=F32) for h in heads]
        for h in heads:
            upd = lax.dot_general(k_st[c][:, ks[h]], v[c][h], _TN, preferred_element_type=F32)
            s_ref[h] = state[h] * jnp.concatenate([dec_t[c][h]] * (DV // LANES), axis=1) + upd
        for h in heads:
            o = o_all[h]
            if reverse:
                o = _rms(o + of_ref[sl, vs[h]], gg_ref[:, vs[h]])
                o_ref[sl, vs[h]] = (o * r_ref[sl, vs[h]].astype(F32)).astype(o_ref.dtype)
            else:
                o_ref[sl, vs[h]] = o


def _gla(proj, z, wa, ba, tables, *, reverse, s_front=None, o_fwd=None, g_gla=None):
    t = proj.shape[0]
    blk, first = tables
    rows = lambda width, col: pl.BlockSpec((GLA_ROWS, width), lambda s, blk_ref, first_ref: (blk_ref[s], col))
    const = lambda shape: pl.BlockSpec(shape, lambda s, blk_ref, first_ref: (0,) * len(shape))
    in_specs = [rows(DQK, 0), rows(DQK, 1), rows(DV_TOT, 1), rows(LANES, 0), const((LANES, DQK)), const((1, DQK))]
    args = [proj, proj, proj, z, wa, ba]
    if reverse:
        in_specs += [rows(DV_TOT, 0), rows(DV_TOT, COL_R), const((1, DV_TOT))]
        args += [o_fwd, proj, g_gla]
    else:
        in_specs += [const((HEADS, DK, DV))]
        args += [s_front]
    return pl.pallas_call(
        functools.partial(_gla_kernel, reverse=reverse),
        grid_spec=pltpu.PrefetchScalarGridSpec(
            num_scalar_prefetch=2,
            grid=(blk.shape[0],),
            in_specs=in_specs,
            out_specs=rows(DV_TOT, 0),
            scratch_shapes=[pltpu.VMEM((HEADS, DK, DV), F32)],
        ),
        out_shape=jax.ShapeDtypeStruct((t, DV_TOT), BF16 if reverse else F32),
        compiler_params=pltpu.CompilerParams(
            dimension_semantics=("arbitrary",),
            vmem_limit_bytes=VMEM_LIMIT),
        name="gla_bwd" if reverse else "gla_fwd",
    )(blk, first, *args)


def _gla_tables(n_seq, seq_len, reverse):
    nb = seq_len // GLA_ROWS
    order = np.arange(nb - 1, -1, -1) if reverse else np.arange(nb)
    blk = np.concatenate([s * nb + order for s in range(n_seq)]).astype(np.int32)
    first = np.tile(np.arange(nb) == 0, n_seq).astype(np.int32)
    return jnp.asarray(blk), jnp.asarray(first)


def _mix_kernel(og_ref, ga_ref, gb_ref, h_ref, g_ref, gp_ref, gn_ref, gf_ref,
                wgo_ref, wco_ref, wout_ref, wdw_ref, bdw_ref, lng_ref, lnb_ref, bco_ref, gpost_ref, gpre_ref,
                h1_ref, u2_ref, g_scr, y_scr, yc_scr, m_scr, *, tm, rb, tiles_per_seq, n_tiles):
    i = pl.program_id(0)
    pos = jnp.minimum(i, n_tiles - 1) % tiles_per_seq
    seq_start = pos == 0
    seq_end = pos == tiles_per_seq - 1

    @pl.when(i == 0)
    def _():
        yc_scr[...] = jnp.zeros_like(yc_scr)

    g_prev = jnp.where(seq_start, gf_ref[...], gp_ref[...]).astype(F32)
    g_next = jnp.where(seq_end, 0.0, gn_ref[...].astype(F32))
    g_cur = g_ref[...].astype(F32)
    for c in range(N_CBLK):
        cs = slice(c * LANES, (c + 1) * LANES)
        g_scr[c, 0:HALO, :] = g_prev[:, cs]
        g_scr[c, HALO:HALO + tm, :] = g_cur[:, cs]
        g_scr[c, HALO + tm:, :] = g_next[:, cs]

    def slab(s, carry):
        for cc in range(CBLK_PER_SLAB):
            c = s * CBLK_PER_SLAB + cc
            for r0 in range(0, tm, rb):
                acc = jnp.zeros((rb, LANES), F32)
                for w in range(CONV_W):
                    start = HALO - CONV_PAD + w + r0
                    acc = acc + g_scr[c, start:start + rb, :] * wdw_ref[c, w:w + 1, :]
                y_scr[c, r0:r0 + rb, :] = acc + bdw_ref[c]
        cols = pl.ds(pl.multiple_of(s * MXU_DIM, MXU_DIM), MXU_DIM)
        y_a = jnp.dot(og_ref[...], wgo_ref[s], preferred_element_type=F32)
        y_b = jnp.dot(yc_scr[...], wco_ref[s], preferred_element_type=F32) + bco_ref[s]
        merged = ga_ref[:, cols].astype(F32) * y_a + gb_ref[:, cols].astype(F32) * y_b
        m_scr[s] = merged.astype(BF16)
        return carry

    lax.fori_loop(0, N_SLAB, slab, 0)

    merged = jnp.concatenate([m_scr[s] for s in range(N_SLAB)], axis=1)
    mix = jnp.dot(merged, wout_ref[...], preferred_element_type=F32)
    h1 = h_ref[...] + _rms(mix, gpost_ref[...])
    h1_ref[...] = h1
    u2_ref[...] = _rms(h1, gpre_ref[...]).astype(BF16)

    s1 = jnp.zeros((tm, LANES), F32)
    for c in range(N_CBLK):
        s1 = s1 + y_scr[c]
    mu = jnp.sum(s1, axis=-1, keepdims=True) / D_MODEL
    s2 = jnp.zeros((tm, LANES), F32)
    for c in range(N_CBLK):
        d = y_scr[c] - mu
        s2 = s2 + d * d
    inv = lax.rsqrt(jnp.sum(s2, axis=-1, keepdims=True) / D_MODEL + EPS)
    for c in range(N_CBLK):
        yn = (y_scr[c] - mu) * inv * lng_ref[c] + lnb_ref[c]
        yc_scr[:, c * LANES:(c + 1) * LANES] = (yn * _sigmoid(yn)).astype(BF16)


def _mix(og, proj, glu, glu_front, h, seq_len, wgo, wco, wout, wdw, bdw, lng, lnb, bco, gpost, gpre, tm, rb):
    t = h.shape[0]
    n_tiles = t // tm
    hb = tm // HALO
    last_hblk = t // HALO - 1
    conv_tile = lambda i: jnp.minimum(i, n_tiles - 1)
    proj_tile = lambda i: jnp.maximum(i - 1, 0)
    ptile = lambda col: pl.BlockSpec((tm, D_MODEL), lambda i: (proj_tile(i), col))
    ctile = pl.BlockSpec((tm, D_MODEL), lambda i: (conv_tile(i), 0))
    prev = pl.BlockSpec((HALO, D_MODEL), lambda i: (jnp.maximum(conv_tile(i) * hb - 1, 0), 0))
    nxt = pl.BlockSpec((HALO, D_MODEL), lambda i: (jnp.minimum((conv_tile(i) + 1) * hb, last_hblk), 0))
    meta = pl.BlockSpec((HALO, D_MODEL), lambda i: (FRONT // HALO, 0))
    resident = lambda shape: pl.BlockSpec(shape, lambda i: (0,) * len(shape), pipeline_mode=pl.Buffered(1))
    vec = resident((1, D_MODEL))
    cvec = resident((N_CBLK, 1, LANES))
    return pl.pallas_call(
        functools.partial(_mix_kernel, tm=tm, rb=rb, tiles_per_seq=seq_len // tm, n_tiles=n_tiles),
        grid=(n_tiles + 1,),
        in_specs=[
            ptile(0), ptile(COL_GATE), ptile(COL_GATE + 1), ptile(0),
            ctile, prev, nxt, meta,
            resident((N_SLAB, DV_TOT, MXU_DIM)), resident((N_SLAB, D_MODEL, MXU_DIM)), resident((D_MODEL, D_MODEL)),
            resident((N_CBLK, CONV_W + 1, LANES)), cvec, cvec, cvec, resident((N_SLAB, 1, MXU_DIM)), vec, vec,
        ],
        out_specs=[ptile(0), ptile(0)],
        out_shape=[jax.ShapeDtypeStruct((t, D_MODEL), F32), jax.ShapeDtypeStruct((t, D_MODEL), BF16)],
        scratch_shapes=[
            pltpu.VMEM((N_CBLK, tm + 2 * HALO, LANES), F32),
            pltpu.VMEM((N_CBLK, tm, LANES), F32),
            pltpu.VMEM((tm, D_MODEL), BF16),
            pltpu.VMEM((N_SLAB, tm, MXU_DIM), BF16),
        ],
        compiler_params=pltpu.CompilerParams(
            dimension_semantics=("arbitrary",),
            vmem_limit_bytes=VMEM_LIMIT),
        name="mix",
    )(og, proj, proj, h, glu, glu, glu, glu_front,
      wgo, wco, wout, wdw, bdw, lng, lnb, bco, gpost, gpre)


def _mlp_kernel(u_ref, h_ref, wu_ref, wd_ref, g_ref, o_ref):
    f = pl.program_id(1)

    @pl.when(f == 0)
    def _():
        o_ref[...] = jnp.zeros_like(o_ref)

    up = jnp.dot(u_ref[...], wu_ref[...], preferred_element_type=F32)
    act = jnp.square(jnp.maximum(up, 0.0)).astype(BF16)
    o_ref[...] += jnp.dot(act, wd_ref[...], preferred_element_type=F32)

    @pl.when(f == pl.num_programs(1) - 1)
    def _():
        o_ref[...] = h_ref[...] + _rms(o_ref[...], g_ref[...])


def _mlp(u2, h1, w_up, w_down, g, tm, tf):
    t = u2.shape[0]
    return pl.pallas_call(
        _mlp_kernel,
        grid=(t // tm, D_FF // tf),
        in_specs=[
            pl.BlockSpec((tm, D_MODEL), lambda i, f: (i, 0)),
            pl.BlockSpec((tm, D_MODEL), lambda i, f: (i, 0)),
            pl.BlockSpec((D_MODEL, tf), lambda i, f: (0, f)),
            pl.BlockSpec((tf, D_MODEL), lambda i, f: (f, 0)),
            pl.BlockSpec((1, D_MODEL), lambda i, f: (0, 0)),
        ],
        out_specs=pl.BlockSpec((tm, D_MODEL), lambda i, f: (i, 0)),
        out_shape=jax.ShapeDtypeStruct((t, D_MODEL), F32),
        compiler_params=pltpu.CompilerParams(
            dimension_semantics=("arbitrary", "arbitrary"),
            vmem_limit_bytes=VMEM_LIMIT),
        name="mlp",
    )(u2, h1, w_up, w_down, g)


def _cblocks(vec):
    return vec.astype(F32).reshape(N_CBLK, 1, LANES)


def _col_slabs(w):
    return w.reshape(w.shape[0], N_SLAB, MXU_DIM).transpose(1, 0, 2).astype(BF16)


def kernel(x_prompt, x_sample, meta_tokens, g_pre_mix, w_in, w_a2_f, b_a_f, w_a2_b, b_a_b, g_gla,
           w_gla_o, w_dw, b_dw, ln_g, ln_b, w_conv_o, b_conv_o, w_out, g_post_mix, g_pre_mlp,
           w_up, w_down, g_post_mlp):
    assert w_in.shape[0] == 1
    l = 0
    tm_proj, tn_proj, tm_mix, rb_mix, tm_mlp, tf_mlp = 1024, 2048, 256, 64, 512, 1024
    row = lambda v: v.astype(F32).reshape(1, -1)
    z0 = N_HEAD_COLS

    w_main = w_in[l].T.astype(BF16)
    w_z = jnp.pad(w_in[l].T[z0:z0 + 2 * GATE_RANK], ((0, LANES - 2 * GATE_RANK), (0, 0))).astype(BF16)
    wa_f = jnp.pad(w_a2_f[l], ((0, LANES - GATE_RANK), (0, 0))).astype(BF16)
    wa_b = jnp.pad(w_a2_b[l], ((GATE_RANK, LANES - 2 * GATE_RANK), (0, 0))).astype(BF16)
    wdw = jnp.pad(w_dw[l].astype(F32), ((0, 1), (0, 0))).reshape(CONV_W + 1, N_CBLK, LANES).transpose(1, 0, 2)
    w_go, w_co, w_o = _col_slabs(w_gla_o[l]), _col_slabs(w_conv_o[l]), w_out[l].astype(BF16)
    b_co = b_conv_o[l].astype(F32).reshape(N_SLAB, 1, MXU_DIM)
    w_u, w_d = w_up[l].astype(BF16), w_down[l].astype(BF16)

    front = jnp.concatenate([jnp.zeros((FRONT, D_MODEL), F32), meta_tokens.astype(F32)], axis=0)
    proj_front, glu_front, z_front = _in_proj(front, row(g_pre_mix[l]), w_main, w_z, CHUNK, tn_proj)
    s_front = _gla_front(proj_front, z_front, wa_f, row(b_a_f[l]))

    outs = []
    for x in (x_prompt, x_sample):
        bsz, seq_len, _ = x.shape
        assert seq_len % GLA_ROWS == 0 and seq_len % tm_mix == 0
        h = x.reshape(bsz * seq_len, D_MODEL)
        proj, glu, z = _in_proj(h, row(g_pre_mix[l]), w_main, w_z, tm_proj, tn_proj)
        o_f = _gla(proj, z, wa_f, row(b_a_f[l]), _gla_tables(bsz, seq_len, False), reverse=False, s_front=s_front)
        og = _gla(proj, z, wa_b, row(b_a_b[l]), _gla_tables(bsz, seq_len, True), reverse=True,
                  o_fwd=o_f, g_gla=row(g_gla[l]))
        h1, u2 = _mix(og, proj, glu, glu_front, h, seq_len, w_go, w_co, w_o, wdw, _cblocks(b_dw[l]),
                      _cblocks(ln_g[l]), _cblocks(ln_b[l]), b_co, row(g_post_mix[l]),
                      row(g_pre_mlp[l]), tm_mix, rb_mix)
        y = _mlp(u2, h1, w_u, w_d, row(g_post_mlp[l]), tm_mlp, tf_mlp)
        outs.append(y.reshape(bsz, seq_len, D_MODEL))
    return tuple(outs)
```
